```python
import math
import jax
import jax.numpy as jnp
from jax import lax
import numpy as np

D_MODEL = 2048
BATCH = 8
SEQ = 2048
DEPTH = 4

HEAD_DIM = 128
N_BRANCH = 4
SB_HEADS = 4
RET_HEADS = 4
RET_CHUNK = 128
DSA_HEADS = 4
DSA_Q_RANK = 384
DSA_TOPK = 256
IDX_HEADS = 8
IDX_DIM = 64
GDN_HEADS = 4
GDN_CONV = 4
GDN_CHUNK = 64
BRANCH_W = 4 * HEAD_DIM
Q_BLOCK = 128
N_GROUPS = 4
EXPERTS_PER_GROUP = 8
N_EXPERTS = N_GROUPS * EXPERTS_PER_GROUP
TOP_K_IN_GROUP = 2
D_EXPERT = 512
MOE_BLOCK = 128
ALPHA = (2.0 * DEPTH) ** 0.25
DEEPNORM_BETA = (8.0 * DEPTH) ** -0.25
NORM_EPS = 1e-5

IN_SPLITS = (3 * BRANCH_W,
             4 * BRANCH_W,
             DSA_Q_RANK,
             2 * HEAD_DIM,
             IDX_DIM,
             IDX_HEADS,
             3 * BRANCH_W,
             GDN_HEADS,
             GDN_HEADS,
             BRANCH_W)
D_IN = sum(IN_SPLITS)
IN_SPLIT_POINTS = tuple(int(p) for p in np.cumsum(IN_SPLITS)[:-1])
DSA_UQ_OUT = DSA_HEADS * HEAD_DIM + IDX_HEADS * IDX_DIM

kernel_name = 'hybrid_gated_four_mixer_hmoe_deepnorm'


def layer_norm(x, g, b):
    xf = x.astype(jnp.float32)
    mu = jnp.mean(xf, -1, keepdims=True)
    var = jnp.mean(jnp.square(xf - mu), -1, keepdims=True)
    return ((xf - mu) * lax.rsqrt(var + NORM_EPS) * g + b).astype(x.dtype)


def rms_norm(x, g):
    xf = x.astype(jnp.float32)
    return (xf * lax.rsqrt(jnp.mean(xf * xf, -1, keepdims=True) + NORM_EPS) * g).astype(x.dtype)


def l2_normalize(x):
    xf = x.astype(jnp.float32)
    return xf * lax.rsqrt(jnp.sum(xf * xf, -1, keepdims=True) + 1e-6)


def to_blocks(t):
    b, l = t.shape[:2]
    return t.reshape(b, l // Q_BLOCK, Q_BLOCK, *t.shape[2:]).swapaxes(0, 1)


def from_blocks(t):
    t = t.swapaxes(0, 1)
    return t.reshape(t.shape[0], t.shape[1] * t.shape[2], *t.shape[3:])


def rotary(x, pos):
    dh = x.shape[-1]
    inv = 1.0 / (10000.0 ** (jnp.arange(0, dh, 2, dtype=jnp.float32) / dh))
    ang = pos.astype(jnp.float32)[:, None] * inv[None, :]
    cos = jnp.cos(ang)[None, :, None, :]
    sin = jnp.sin(ang)[None, :, None, :]
    x1 = x[..., : dh // 2].astype(jnp.float32)
    x2 = x[..., dh // 2:].astype(jnp.float32)
    return jnp.concatenate([x1 * cos - x2 * sin, x1 * sin + x2 * cos], -1).astype(x.dtype)


def stick_breaking_attention(q, k, v):
    B, L, H, dh = q.shape
    nb = L // Q_BLOCK
    scale = dh ** -0.5
    kpos = jnp.arange(L)

    def block(args):
        qi, i = args
        z = jnp.einsum('bqhd,bkhd->bhqk', qi, k, preferred_element_type=jnp.float32) * scale
        qpos = i * Q_BLOCK + jnp.arange(Q_BLOCK)
        causal = kpos[None, :] < qpos[:, None]
        log_beta = jax.nn.log_sigmoid(z)
        log_1m = jnp.where(causal, jax.nn.log_sigmoid(-z), 0.0)
        suffix = lax.cumsum(log_1m, axis=3, reverse=True) - log_1m
        w = jnp.where(causal, jnp.exp(log_beta + suffix), 0.0)
        return jnp.einsum('bhqk,bkhd->bqhd', w.astype(v.dtype), v)

    out = lax.map(block, (to_blocks(q), jnp.arange(nb)))
    return from_blocks(out).reshape(B, L, H * dh)


def retention(q, k, v):
    B, L, H, dh = q.shape
    C = RET_CHUNK
    n = L // C
    f32 = jnp.float32
    log_gamma = jnp.log1p(-jnp.exp2(-5.0 - jnp.arange(H, dtype=f32)))
    pos = jnp.arange(C, dtype=f32)
    rel = pos[:, None] - pos[None, :]
    causal = rel >= 0
    d_intra = jnp.where(causal, jnp.exp(jnp.where(causal, rel, 0.0) * log_gamma[:, None, None]), 0.0)
    d_q = jnp.exp((pos + 1.0) * log_gamma[:, None])
    d_k = jnp.exp((C - 1.0 - pos) * log_gamma[:, None])
    d_chunk = jnp.exp(C * log_gamma)
    ch = lambda t: t.astype(f32).reshape(B, n, C, H, dh)
    qc, kc, vc = ch(q), ch(k) * dh ** -0.5, ch(v)
    s = jnp.einsum('bnihd,bnjhd->bnhij', qc, kc) * d_intra
    o = jnp.einsum('bnhij,bnjhe->bnihe', s, vc)
    kv = jnp.einsum('bnjhd,hj,bnjhe->nbhde', kc, d_k, vc)

    def step(state, kv_c):
        return d_chunk[None, :, None, None] * state + kv_c, state

    _, s_prev = lax.scan(step, jnp.zeros((B, H, dh, dh), f32), kv)
    o = o + jnp.einsum('bnihd,nbhde,hi->bnihe', qc, s_prev, d_q)
    mu = jnp.mean(o, -1, keepdims=True)
    var = jnp.mean(jnp.square(o - mu), -1, keepdims=True)
    o = (o - mu) * lax.rsqrt(var + NORM_EPS)
    return o.reshape(B, L, H * dh)


def dsa_attention(q, k, v, q_idx, k_idx, w_idx):
    B, L, H, dh = q.shape
    topk = min(DSA_TOPK, L // 4)
    nb = L // Q_BLOCK
    kpos = jnp.arange(L)
    gather_rows = jax.vmap(lambda tb, ib: tb[ib])

    def block(args):
        qi, qii, wi, i = args
        qpos = i * Q_BLOCK + jnp.arange(Q_BLOCK)
        causal = kpos[None, :] <= qpos[:, None]
        s = jnp.einsum('bqhd,bkd->bqhk', qii, k_idx, preferred_element_type=jnp.float32) * IDX_DIM ** -0.5
        score = jnp.einsum('bqh,bqhk->bqk', wi.astype(jnp.float32), jax.nn.relu(s))
        score = jnp.where(causal[None], score, -jnp.inf)
        _, sel = lax.top_k(score, topk)
        valid = sel <= qpos[None, :, None]
        ks = gather_rows(k, sel)
        vs = gather_rows(v, sel)
        logits = jnp.einsum('bqhd,bqkd->bqhk', qi, ks, preferred_element_type=jnp.float32) * dh ** -0.5
        logits = jnp.where(valid[:, :, None, :], logits, -jnp.inf)
        p = jax.nn.softmax(logits, axis=-1)
        return jnp.einsum('bqhk,bqkd->bqhd', p.astype(vs.dtype), vs)

    out = lax.map(block, (to_blocks(q), to_blocks(q_idx), to_blocks(w_idx), jnp.arange(nb)))
    return from_blocks(out).reshape(B, L, H * dh)


def causal_depthwise_conv(x, w):
    width = w.shape[0]
    return lax.conv_general_dilated(x, w[:, None, :], window_strides=(1,), padding=[(width - 1, 0)],
                                    dimension_numbers=('NWC', 'WIO', 'NWC'),
                                    feature_group_count=x.shape[-1])


def gated_deltanet(q, k, v, beta, g):
    B, L, H, dk = q.shape
    dv = v.shape[-1]
    C = GDN_CHUNK
    n = L // C
    f32 = jnp.float32
    ch = lambda t: t.astype(f32).reshape(B, n, C, H, -1).transpose(0, 3, 1, 2, 4)
    qc, kc, vc = ch(q) * dk ** -0.5, ch(k), ch(v)
    bc = beta.astype(f32).reshape(B, n, C, H).transpose(0, 3, 1, 2)[..., None]
    gc = jnp.cumsum(g.astype(f32).reshape(B, n, C, H).transpose(0, 3, 1, 2), axis=-1)
    tri_incl = jnp.tril(jnp.ones((C, C), bool))
    tri_strict = jnp.tril(jnp.ones((C, C), bool), -1)
    diff = gc[..., :, None] - gc[..., None, :]
    decay_mat = jnp.where(tri_incl, jnp.exp(jnp.where(tri_incl, diff, 0.0)), 0.0)
    k_beta = kc * bc
    a = jnp.where(tri_strict, jnp.einsum('bhnid,bhnjd->bhnij', k_beta, kc) * decay_mat, 0.0)
    rhs = jnp.concatenate([vc * bc, k_beta * jnp.exp(gc)[..., None]], axis=-1)
    sol = lax.linalg.triangular_solve(jnp.eye(C, dtype=f32) + a, rhs, left_side=True,
                                      lower=True, unit_diagonal=True)
    u, w = sol[..., :dv], sol[..., dv:]
    qk = jnp.where(tri_incl, jnp.einsum('bhnid,bhnjd->bhnij', qc, kc) * decay_mat, 0.0)
    q_dec = qc * jnp.exp(gc)[..., None]
    k_dec = kc * jnp.exp(gc[..., -1:] - gc)[..., None]
    c_dec = jnp.exp(gc[..., -1])

    def step(state, xs):
        u_c, w_c, qk_c, qd_c, kd_c, cd_c = xs
        v_new = u_c - jnp.einsum('bhcd,bhde->bhce', w_c, state)
        o = jnp.einsum('bhcd,bhde->bhce', qd_c, state) + jnp.einsum('bhij,bhje->bhie', qk_c, v_new)
        state = state * cd_c[..., None, None] + jnp.einsum('bhcd,bhce->bhde', kd_c, v_new)
        return state, o

    xs = tuple(jnp.moveaxis(t, 2, 0) for t in (u, w, qk, q_dec, k_dec, c_dec))
    _, o = lax.scan(step, jnp.zeros((B, H, dk, dv), f32), xs)
    return o.transpose(1, 0, 3, 2, 4).reshape(B, L, H, dv)


def hybrid_mixer(x, w_in, dsa_q_norm, w_uq, gdn_conv, gdn_a_log, gdn_dt_bias, gdn_norm,
                 w_branch, w_branch_gate, w_out):
    B, L, _ = x.shape
    f32 = jnp.float32
    heads = lambda t, h: t.reshape(B, L, h, -1)
    (sb_qkv, ret_qkvg, dsa_cq, dsa_kv, idx_k, idx_w,
     gdn_qkv, gdn_b, gdn_a, gdn_g) = jnp.split(x @ w_in, IN_SPLIT_POINTS, axis=-1)
    q, k, v = jnp.split(sb_qkv, 3, axis=-1)
    y_a = stick_breaking_attention(heads(q, SB_HEADS), heads(k, SB_HEADS), heads(v, SB_HEADS))
    q, k, v, g = jnp.split(ret_qkvg, 4, axis=-1)
    pos = jnp.arange(L)
    y_b = retention(rotary(heads(q, RET_HEADS), pos), rotary(heads(k, RET_HEADS), pos), heads(v, RET_HEADS))
    y_b = y_b.astype(x.dtype) * jax.nn.silu(g)
    q_all = rms_norm(dsa_cq, dsa_q_norm) @ w_uq
    q, q_idx = jnp.split(q_all, [DSA_HEADS * HEAD_DIM], axis=-1)
    k_c, v_c = jnp.split(dsa_kv, 2, axis=-1)
    y_c = dsa_attention(heads(q, DSA_HEADS), k_c, v_c, heads(q_idx, IDX_HEADS), idx_k,
                        idx_w * IDX_HEADS ** -0.5)
    q, k, v = jnp.split(jax.nn.silu(causal_depthwise_conv(gdn_qkv, gdn_conv)), 3, axis=-1)
    beta = jax.nn.sigmoid(gdn_b.astype(f32))
    log_decay = -jnp.exp(gdn_a_log.astype(f32)) * jax.nn.softplus(gdn_a.astype(f32) + gdn_dt_bias.astype(f32))
    o = gated_deltanet(l2_normalize(heads(q, GDN_HEADS)), l2_normalize(heads(k, GDN_HEADS)),
                       heads(v, GDN_HEADS), beta, log_decay)
    y_d = (rms_norm(o, gdn_norm) * jax.nn.silu(heads(gdn_g, GDN_HEADS).astype(f32))).reshape(B, L, -1)
    merged = None
    for i, y in enumerate((y_a, y_b, y_c, y_d)):
        term = jax.nn.sigmoid(x @ w_branch_gate[i]) * (y.astype(x.dtype) @ w_branch[i])
        merged = term if merged is None else merged + term
    return merged @ w_out


def hierarchical_moe(x, w_router_group, b_router_group, w_router_expert, b_router_expert,
                     w_exp_gate, w_exp_up, w_exp_down):
    B, L, D = x.shape
    T = B * L
    f32 = jnp.float32
    xt = x.reshape(T, D)
    g_logits = jnp.dot(xt, w_router_group, preferred_element_type=f32) + b_router_group.astype(f32)
    grp = jnp.argmax(g_logits, axis=-1)
    p_grp = jnp.take_along_axis(jax.nn.softmax(g_logits, -1), grp[:, None], axis=-1)
    e_logits = (jnp.dot(xt, w_router_expert, preferred_element_type=f32)
                + b_router_expert.astype(f32)).reshape(T, N_GROUPS, EXPERTS_PER_GROUP)
    e_logits = jnp.take_along_axis(e_logits, grp[:, None, None], axis=1)[:, 0]
    top_v, top_i = lax.top_k(e_logits, TOP_K_IN_GROUP)
    gate = p_grp * jax.nn.softmax(top_v, -1)
    expert = (grp[:, None] * EXPERTS_PER_GROUP + top_i).reshape(-1).astype(jnp.int32)
    n_assign = T * TOP_K_IN_GROUP
    order = jnp.argsort(expert)
    sorted_e = expert[order]
    tok = (order // TOP_K_IN_GROUP).astype(jnp.int32)
    sizes = jnp.bincount(expert, length=N_EXPERTS)
    padded = (sizes + MOE_BLOCK - 1) // MOE_BLOCK * MOE_BLOCK
    starts = jnp.cumsum(sizes) - sizes
    pad_ends = jnp.cumsum(padded)
    pad_starts = pad_ends - padded
    slot = pad_starts[sorted_e] + jnp.arange(n_assign) - starts[sorted_e]
    n_blocks = -(-n_assign // MOE_BLOCK) + N_EXPERTS
    n_slots = n_blocks * MOE_BLOCK
    slot_tok = jnp.full((n_slots,), T, jnp.int32).at[slot].set(tok)
    slot_gate = jnp.zeros((n_slots,), f32).at[slot].set(gate.reshape(-1)[order])
    block_e = jnp.minimum(jnp.searchsorted(pad_ends, jnp.arange(n_blocks) * MOE_BLOCK, side='right'),
                          N_EXPERTS - 1)
    x_pad = jnp.concatenate([xt, jnp.zeros((1, D), xt.dtype)], axis=0)
    xb = x_pad[slot_tok].reshape(n_blocks, MOE_BLOCK, D)

    def expert_block(args):
        xe, e = args
        h = jax.nn.silu(xe @ w_exp_gate[e]) * (xe @ w_exp_up[e])
        return h @ w_exp_down[e]

    yb = lax.map(expert_block, (xb, block_e)).reshape(n_slots, D)
    yb = yb * slot_gate[:, None].astype(yb.dtype)
    out = jnp.zeros((T + 1, D), yb.dtype).at[slot_tok].add(yb)[:T]
    return out.reshape(B, L, D)


def setup_inputs(seed: int = 0) -> dict:
    key = jax.random.key(seed)
    ks = jax.random.split(key, 24)
    f32 = jnp.float32
    D = D_MODEL
    nrm = lambda k, shape, scale: jax.random.normal(k, shape, f32) * scale
    dt = jnp.exp(jax.random.uniform(ks[6], (DEPTH, GDN_HEADS), f32, math.log(1e-3), math.log(1e-1)))
    return {
        'x': nrm(ks[0], (BATCH, SEQ, D), 1.0),
        'w_in': nrm(ks[1], (DEPTH, D, D_IN), D ** -0.5),
        'dsa_q_norm': 1.0 + nrm(ks[2], (DEPTH, DSA_Q_RANK), 0.02),
        'w_uq': nrm(ks[3], (DEPTH, DSA_Q_RANK, DSA_UQ_OUT), DSA_Q_RANK ** -0.5),
        'gdn_conv': nrm(ks[4], (DEPTH, GDN_CONV, 3 * BRANCH_W), GDN_CONV ** -0.5),
        'gdn_a_log': jnp.log(jax.random.uniform(ks[5], (DEPTH, GDN_HEADS), f32, 1.0, 16.0)),
        'gdn_dt_bias': dt + jnp.log(-jnp.expm1(-dt)),
        'gdn_norm': 1.0 + nrm(ks[7], (DEPTH, HEAD_DIM), 0.02),
        'w_branch': nrm(ks[8], (DEPTH, N_BRANCH, BRANCH_W, D), BRANCH_W ** -0.5),
        'w_branch_gate': nrm(ks[9], (DEPTH, N_BRANCH, D, D), D ** -0.5),
        'w_out': nrm(ks[10], (DEPTH, D, D), D ** -0.5 * DEEPNORM_BETA),
        'ln1_g': 1.0 + nrm(ks[11], (DEPTH, D), 0.02),
        'ln1_b': nrm(ks[12], (DEPTH, D), 0.02),
        'w_router_group': nrm(ks[13], (DEPTH, D, N_GROUPS), D ** -0.5),
        'b_router_group': nrm(ks[14], (DEPTH, N_GROUPS), 0.01),
        'w_router_expert': nrm(ks[15], (DEPTH, D, N_EXPERTS), D ** -0.5),
        'b_router_expert': nrm(ks[16], (DEPTH, N_EXPERTS), 0.01),
        'w_exp_gate': nrm(ks[17], (DEPTH, N_EXPERTS, D, D_EXPERT), D ** -0.5),
        'w_exp_up': nrm(ks[18], (DEPTH, N_EXPERTS, D, D_EXPERT), D ** -0.5),
        'w_exp_down': nrm(ks[19], (DEPTH, N_EXPERTS, D_EXPERT, D), D_EXPERT ** -0.5 * DEEPNORM_BETA),
        'ln2_g': 1.0 + nrm(ks[20], (DEPTH, D), 0.02),
        'ln2_b': nrm(ks[21], (DEPTH, D), 0.02),
    }


def reference(x, w_in, dsa_q_norm, w_uq, gdn_conv, gdn_a_log, gdn_dt_bias, gdn_norm,
              w_branch, w_branch_gate, w_out, ln1_g, ln1_b, w_router_group, b_router_group,
              w_router_expert, b_router_expert, w_exp_gate, w_exp_up, w_exp_down, ln2_g, ln2_b):
    for l in range(DEPTH):
        y = hybrid_mixer(x, w_in[l], dsa_q_norm[l], w_uq[l], gdn_conv[l], gdn_a_log[l],
                         gdn_dt_bias[l], gdn_norm[l], w_branch[l], w_branch_gate[l], w_out[l])
        x = layer_norm(ALPHA * x + y, ln1_g[l], ln1_b[l])
        y = hierarchical_moe(x, w_router_group[l], b_router_group[l], w_router_expert[l],
                             b_router_expert[l], w_exp_gate[l], w_exp_up[l], w_exp_down[l])
        x = layer_norm(ALPHA * x + y, ln2_g[l], ln2_b[l])
    return x
```

```python
import functools
import math

import numpy as np
import jax
import jax.numpy as jnp
from jax import lax
from jax.experimental import pallas as pl
from jax.experimental.pallas import tpu as pltpu

F32 = jnp.float32
BF16 = jnp.bfloat16
I32 = jnp.int32

D_MODEL = 2048
DEPTH = 4
HEAD_DIM = 128
N_HEADS = 4
BRANCH_W = N_HEADS * HEAD_DIM
RET_CHUNK = 128
DSA_Q_RANK = 384
DSA_TOPK = 256
IDX_HEADS = 8
IDX_DIM = 64
GDN_CONV = 4
GDN_CHUNK = 64
N_GROUPS = 4
EXPERTS_PER_GROUP = 8
N_EXPERTS = N_GROUPS * EXPERTS_PER_GROUP
D_EXPERT = 512
ALPHA = (2.0 * DEPTH) ** 0.25
NORM_EPS = 1e-5

LANES = 128
SUBLANES = 8
VMEM_LIMIT_BYTES = 56 * 1024 * 1024

P16_W = 2048
P16_DSA_K_BLK = 12
P16_DSA_V_BLK = 13
P16_IDXK_BLK = 14
P32_W = 4608
P32_SMALL_BLK = 35
SMALL_IDXW, SMALL_B, SMALL_A = 0, 8, 12
UQ_W = IDX_HEADS * LANES + BRANCH_W
UQ_Q_BLK = IDX_HEADS * LANES // BRANCH_W

INT_MIN = -2147483648


def _cparams(sem):
    return pltpu.CompilerParams(dimension_semantics=sem, vmem_limit_bytes=VMEM_LIMIT_BYTES)


def _bf(x):
    return x.astype(BF16)


def _dot(a, b):
    return jnp.dot(a, b, preferred_element_type=F32)


def _dot_nt(a, b):
    return lax.dot_general(a, b, (((1,), (1,)), ((), ())), preferred_element_type=F32)


def _dot_tn(a, b):
    return lax.dot_general(a, b, (((0,), (0,)), ((), ())), preferred_element_type=F32)


def _split2(x):
    hi = _bf(x)
    lo = _bf(x - hi.astype(F32))
    return hi, lo


def _split3(x):
    hi = _bf(x)
    r = x - hi.astype(F32)
    mid = _bf(r)
    lo = _bf(r - mid.astype(F32))
    return hi, mid, lo


def _dot_left_exact(m01, x):
    hi, mid, lo = _split3(x)
    return _dot(m01, hi) + _dot(m01, mid) + _dot(m01, lo)


def _dot_right_exact(x, m01):
    hi, mid, lo = _split3(x)
    return _dot(hi, m01) + _dot(mid, m01) + _dot(lo, m01)


def _dot_hp(a, b):
    ah, al = _split2(a)
    bh, bl = _split2(b)
    return _dot(ah, bh) + _dot(ah, bl) + _dot(al, bh)


def _silu(x):
    return x * (1.0 / (1.0 + jnp.exp(-x)))


def _sigmoid(x):
    return 1.0 / (1.0 + jnp.exp(-x))


def _softplus(x):
    return jnp.maximum(x, 0.0) + jnp.log1p(jnp.exp(-jnp.abs(x)))


def _mm_kernel(x_ref, w_ref, o_ref):
    o_ref[...] = _dot(x_ref[...], w_ref[...]).astype(o_ref.dtype)


def _matmul(x, w, out_dtype, tm, tn, name):
    m, k = x.shape
    n = w.shape[1]
    tm = min(tm, m)
    return pl.pallas_call(
        _mm_kernel,
        grid=(m // tm, n // tn),
        in_specs=[pl.BlockSpec((tm, k), lambda i, j: (i, 0)),
                  pl.BlockSpec((k, tn), lambda i, j: (0, j))],
        out_specs=pl.BlockSpec((tm, tn), lambda i, j: (i, j)),
        out_shape=jax.ShapeDtypeStruct((m, n), out_dtype),
        compiler_params=_cparams(("parallel", "arbitrary")),
        name=name,
    )(x, w)


def _sb_kernel(q_ref, k_ref, v_ref, o_ref, *, tq):
    i = pl.program_id(2)
    q = q_ref[...]
    scale = HEAD_DIM ** -0.5
    row = i * tq + lax.broadcasted_iota(I32, (tq, 1), 0)
    col0 = lax.broadcasted_iota(I32, (1, tq), 1)
    jr = lax.broadcasted_iota(I32, (tq, tq), 0)
    jc = lax.broadcasted_iota(I32, (tq, tq), 1)
    later = jnp.where(jr > jc, 1.0, 0.0).astype(BF16)

    def body(jj, carry):
        acc, run = carry
        kb = i - jj
        ks = pl.multiple_of(kb * tq, tq)
        kblk = k_ref[pl.ds(ks, tq), :]
        vblk = v_ref[pl.ds(ks, tq), :]
        z = _dot_nt(q, kblk) * scale
        causal = (kb * tq + col0) < row
        sp = jnp.log1p(jnp.exp(-jnp.abs(z)))
        log_beta = jnp.minimum(z, 0.0) - sp
        log_1m = jnp.where(causal, log_beta - z, 0.0)
        hi, lo = _split2(log_1m)
        suffix = _dot(hi, later) + _dot(lo, later)
        w = jnp.where(causal, jnp.exp(log_beta + suffix + run), 0.0)
        acc = acc + _dot(_bf(w), vblk)
        run = run + suffix[:, 0:1] + log_1m[:, 0:1]
        return acc, run

    acc, _ = lax.fori_loop(0, i + 1, body,
                           (jnp.zeros((tq, HEAD_DIM), F32), jnp.zeros((tq, 1), F32)))
    o_ref[...] = acc.astype(o_ref.dtype)


def _sb_attention(p16, batch, seq):
    tq = min(256, seq)
    nq = seq // tq
    t = batch * seq
    return pl.pallas_call(
        functools.partial(_sb_kernel, tq=tq),
        grid=(batch, N_HEADS, nq),
        in_specs=[pl.BlockSpec((tq, HEAD_DIM), lambda b, h, i: (b * nq + i, h)),
                  pl.BlockSpec((seq, HEAD_DIM), lambda b, h, i: (b, N_HEADS + h)),
                  pl.BlockSpec((seq, HEAD_DIM), lambda b, h, i: (b, 2 * N_HEADS + h))],
        out_specs=pl.BlockSpec((tq, HEAD_DIM), lambda b, h, i: (b * nq + i, h)),
        out_shape=jax.ShapeDtypeStruct((t, BRANCH_W), BF16),
        compiler_params=_cparams(("parallel", "parallel", "arbitrary")),
        name="sb_attention",
    )(p16, p16, p16)


def _ret_kernel(dch_ref, q_ref, k_ref, v_ref, g_ref, cos_ref, sin_ref, dintra_ref, dq_ref, dk_ref,
                o_ref, state_ref, *, n_chunks):
    c = RET_CHUNK

    @pl.when(pl.program_id(1) == 0)
    def _():
        state_ref[...] = jnp.zeros_like(state_ref)

    def rot(x, cs, sn):
        return x * cs + pltpu.roll(x, HEAD_DIM // 2, 1) * sn

    for h in range(N_HEADS):
        hs = slice(h * HEAD_DIM, (h + 1) * HEAD_DIM)
        state = state_ref[h]
        for ci in range(n_chunks):
            rs = slice(ci * c, (ci + 1) * c)
            cs = cos_ref[rs, :]
            sn = sin_ref[rs, :]
            qc = rot(q_ref[rs, hs], cs, sn)
            kc = rot(k_ref[rs, hs], cs, sn) * (HEAD_DIM ** -0.5)
            vc = _bf(v_ref[rs, hs])
            qb = _bf(qc)
            s = _dot_nt(qb, _bf(kc)) * dintra_ref[h]
            o = _dot(_bf(s), vc) + _dot(qb, _bf(state)) * dq_ref[h]
            kv = _dot_tn(_bf(kc * dk_ref[h]), vc)
            state = dch_ref[h] * state + kv
            mu = jnp.mean(o, axis=-1, keepdims=True)
            d = o - mu
            var = jnp.mean(d * d, axis=-1, keepdims=True)
            y = d * lax.rsqrt(var + NORM_EPS)
            o_ref[rs, hs] = (y * _silu(g_ref[rs, hs])).astype(o_ref.dtype)
        state_ref[h] = state


def _retention(p32, batch, seq):
    lb = min(512, seq)
    nb = seq // lb
    t = batch * seq
    f32 = F32
    hh = N_HEADS
    c = RET_CHUNK
    log_gamma = jnp.log1p(-jnp.exp2(-5.0 - jnp.arange(hh, dtype=f32)))
    pos = jnp.arange(c, dtype=f32)
    rel = pos[:, None] - pos[None, :]
    causal = rel >= 0
    d_intra = jnp.where(causal, jnp.exp(jnp.where(causal, rel, 0.0) * log_gamma[:, None, None]), 0.0)
    d_q = jnp.exp((pos + 1.0) * log_gamma[:, None])
    d_k = jnp.exp((c - 1.0 - pos) * log_gamma[:, None])
    d_chunk = jnp.exp(c * log_gamma)
    dq_b = jnp.broadcast_to(d_q[:, :, None], (hh, c, HEAD_DIM))
    dk_b = jnp.broadcast_to(d_k[:, :, None], (hh, c, HEAD_DIM))
    inv = 1.0 / (10000.0 ** (jnp.arange(0, HEAD_DIM, 2, dtype=f32) / HEAD_DIM))
    ang = jnp.arange(seq).astype(f32)[:, None] * inv[None, :]
    cos_t = jnp.concatenate([jnp.cos(ang), jnp.cos(ang)], -1)
    sin_t = jnp.concatenate([-jnp.sin(ang), jnp.sin(ang)], -1)

    qkvg = lambda blk: pl.BlockSpec((lb, BRANCH_W), lambda b, j, blk=blk: (b * nb + j, blk))
    tab = pl.BlockSpec((lb, HEAD_DIM), lambda b, j: (j, 0))
    whole3 = pl.BlockSpec((hh, c, HEAD_DIM), lambda b, j: (0, 0, 0))
    return pl.pallas_call(
        functools.partial(_ret_kernel, n_chunks=lb // c),
        grid=(batch, nb),
        in_specs=[pl.BlockSpec(memory_space=pltpu.SMEM),
                  qkvg(3), qkvg(4), qkvg(5), qkvg(6), tab, tab, whole3, whole3, whole3],
        out_specs=pl.BlockSpec((lb, BRANCH_W), lambda b, j: (b * nb + j, 0)),
        out_shape=jax.ShapeDtypeStruct((t, BRANCH_W), BF16),
        scratch_shapes=[pltpu.VMEM((hh, HEAD_DIM, HEAD_DIM), F32)],
        compiler_params=_cparams(("parallel", "arbitrary")),
        name="retention",
    )(d_chunk, p32, p32, p32, p32, cos_t, sin_t, d_intra, dq_b, dk_b)


def _uq_kernel(x_ref, g_ref, w_ref, o_ref):
    x = x_ref[:, :DSA_Q_RANK]
    ms = jnp.mean(x * x, axis=-1, keepdims=True)
    xn = x * lax.rsqrt(ms + NORM_EPS) * g_ref[...]
    o_ref[...] = _dot(_bf(xn), w_ref[...]).astype(o_ref.dtype)


def _dsa_query(p32, q_norm, w_uq16):
    t = p32.shape[0]
    tm = min(512, t)
    return pl.pallas_call(
        _uq_kernel,
        grid=(t // tm,),
        in_specs=[pl.BlockSpec((tm, 512), lambda i: (i, 8)),
                  pl.BlockSpec((1, DSA_Q_RANK), lambda i: (0, 0)),
                  pl.BlockSpec((DSA_Q_RANK, UQ_W), lambda i: (0, 0))],
        out_specs=pl.BlockSpec((tm, UQ_W), lambda i: (i, 0)),
        out_shape=jax.ShapeDtypeStruct((t, UQ_W), BF16),
        compiler_params=_cparams(("parallel",)),
        name="dsa_query",
    )(p32, q_norm, w_uq16)


def _dsa_kernel(q_ref, qi_ref, small_ref, kidx_ref, k_ref, v_ref, o_ref, keys_ref,
                *, tq, kc, topk):
    i = pl.program_id(1)
    nkv = ((i + 1) * tq + kc - 1) // kc
    row = i * tq + lax.broadcasted_iota(I32, (tq, 1), 0)
    col0 = lax.broadcasted_iota(I32, (1, kc), 1)
    wrow = small_ref[...] * (IDX_HEADS ** -0.5) * (IDX_DIM ** -0.5)
    wcols = [wrow[:, SMALL_IDXW + h:SMALL_IDXW + h + 1] for h in range(IDX_HEADS)]
    qis = [qi_ref[:, h * LANES:(h + 1) * LANES] for h in range(IDX_HEADS)]

    def score_chunk(j, _):
        ks = pl.multiple_of(j * kc, kc)
        kix = kidx_ref[pl.ds(ks, kc), :]
        acc = jnp.zeros((tq, kc), F32)
        for h in range(IDX_HEADS):
            acc = acc + jnp.maximum(_dot_nt(qis[h], kix), 0.0) * wcols[h]
        bits = pltpu.bitcast(acc, I32)
        key = jnp.where(bits < 0, bits ^ 0x7FFFFFFF, bits)
        causal = (j * kc + col0) <= row
        keys_ref[j] = jnp.where(causal, key, INT_MIN)
        return 0

    lax.fori_loop(0, nkv, score_chunk, 0)

    def count_ge(cand):
        def cbody(j, cnt):
            return cnt + jnp.sum(jnp.where(keys_ref[j] >= cand, 1.0, 0.0), axis=-1, keepdims=True)
        return lax.fori_loop(0, nkv, cbody, jnp.zeros((tq, 1), F32))

    def bit_step(it, theta):
        cand = theta + lax.shift_left(jnp.int32(1), 31 - it)
        return jnp.where(count_ge(cand) >= topk, cand, theta)

    theta = lax.fori_loop(0, 32, bit_step, jnp.full((tq, 1), INT_MIN, I32))
    n_gt = count_ge(theta + 1)
    n_gt = jnp.where(theta == 2147483647, 0.0, n_gt)
    need = topk - n_gt

    ur = lax.broadcasted_iota(I32, (kc, kc), 0)
    uc = lax.broadcasted_iota(I32, (kc, kc), 1)
    upto = jnp.where(ur <= uc, 1.0, 0.0).astype(BF16)
    qs = [q_ref[:, h * HEAD_DIM:(h + 1) * HEAD_DIM] for h in range(N_HEADS)]
    scale = HEAD_DIM ** -0.5
    neg = -1e30

    def attend(j, carry):
        seen, ms, ls, accs = carry
        ks = pl.multiple_of(j * kc, kc)
        key = keys_ref[j]
        causal = (j * kc + col0) <= row
        gt = (key > theta) & causal
        eq = key == theta
        eqf = jnp.where(eq, 1.0, 0.0).astype(BF16)
        pc = _dot(eqf, upto) + seen
        sel = gt | (eq & causal & (pc <= need))
        seen = pc[:, kc - 1:kc]
        kblk = k_ref[pl.ds(ks, kc), :]
        vblk = v_ref[pl.ds(ks, kc), :]
        ms2, ls2, accs2 = [], [], []
        for h in range(N_HEADS):
            logit = jnp.where(sel, _dot_nt(qs[h], kblk) * scale, neg)
            m_new = jnp.maximum(ms[h], jnp.max(logit, axis=-1, keepdims=True))
            p = jnp.where(sel, jnp.exp(logit - m_new), 0.0)
            a = jnp.exp(ms[h] - m_new)
            ls2.append(a * ls[h] + jnp.sum(p, axis=-1, keepdims=True))
            accs2.append(a * accs[h] + _dot(_bf(p), vblk))
            ms2.append(m_new)
        return seen, tuple(ms2), tuple(ls2), tuple(accs2)

    init = (jnp.zeros((tq, 1), F32),
            tuple(jnp.full((tq, 1), neg, F32) for _ in range(N_HEADS)),
            tuple(jnp.zeros((tq, 1), F32) for _ in range(N_HEADS)),
            tuple(jnp.zeros((tq, HEAD_DIM), F32) for _ in range(N_HEADS)))
    _, _, ls, accs = lax.fori_loop(0, nkv, attend, init)
    for h in range(N_HEADS):
        o_ref[:, h * HEAD_DIM:(h + 1) * HEAD_DIM] = (accs[h] / ls[h]).astype(o_ref.dtype)


def _dsa_attention(q_all, p32, p16, batch, seq):
    tq = min(256, seq)
    kc = min(512, seq)
    nq = seq // tq
    t = batch * seq
    topk = min(DSA_TOPK, seq // 4)
    kv = lambda blk: pl.BlockSpec((seq, LANES), lambda b, i, blk=blk: (b, blk))
    return pl.pallas_call(
        functools.partial(_dsa_kernel, tq=tq, kc=kc, topk=topk),
        grid=(batch, nq),
        in_specs=[pl.BlockSpec((tq, BRANCH_W), lambda b, i: (b * nq + i, UQ_Q_BLK)),
                  pl.BlockSpec((tq, IDX_HEADS * LANES), lambda b, i: (b * nq + i, 0)),
                  pl.BlockSpec((tq, LANES), lambda b, i: (b * nq + i, P32_SMALL_BLK)),
                  kv(P16_IDXK_BLK), kv(P16_DSA_K_BLK), kv(P16_DSA_V_BLK)],
        out_specs=pl.BlockSpec((tq, BRANCH_W), lambda b, i: (b * nq + i, 0)),
        out_shape=jax.ShapeDtypeStruct((t, BRANCH_W), BF16),
        scratch_shapes=[pltpu.VMEM((seq // kc, tq, kc), I32)],
        compiler_params=_cparams(("parallel", "arbitrary")),
        name="dsa_attention",
    )(q_all, q_all, p32, p16, p16, p16)


def _conv_kernel(prev_ref, x_ref, w_ref, o_ref, *, tl):
    first = pl.program_id(1) == 0
    prev = jnp.where(first, 0.0, prev_ref[...])
    ext = jnp.concatenate([prev, x_ref[...]], axis=0)
    acc = ext[SUBLANES:, :] * w_ref[GDN_CONV - 1:GDN_CONV, :]
    for d in range(1, GDN_CONV):
        acc = acc + pltpu.roll(ext, d, 0)[SUBLANES:, :] * w_ref[GDN_CONV - 1 - d:GDN_CONV - d, :]
    y = _silu(acc)
    is_v = pl.program_id(2) == 2
    for h in range(N_HEADS):
        hs = slice(h * HEAD_DIM, (h + 1) * HEAD_DIM)
        yh = y[:, hs]
        nrm = lax.rsqrt(jnp.sum(yh * yh, axis=-1, keepdims=True) + 1e-6)
        o_ref[:, hs] = yh * jnp.where(is_v, 1.0, nrm)


def _gdn_conv(p32, conv_w, batch, seq):
    tl = min(256, seq)
    nl = seq // tl
    t = batch * seq
    per8 = tl // SUBLANES
    return pl.pallas_call(
        functools.partial(_conv_kernel, tl=tl),
        grid=(batch, nl, 3),
        in_specs=[pl.BlockSpec((SUBLANES, BRANCH_W),
                               lambda b, i, c: (jnp.maximum((b * nl + i) * per8 - 1, 0), c)),
                  pl.BlockSpec((tl, BRANCH_W), lambda b, i, c: (b * nl + i, c)),
                  pl.BlockSpec((GDN_CONV, BRANCH_W), lambda b, i, c: (0, c))],
        out_specs=pl.BlockSpec((tl, BRANCH_W), lambda b, i, c: (b * nl + i, c)),
        out_shape=jax.ShapeDtypeStruct((t, 3 * BRANCH_W), F32),
        compiler_params=_cparams(("parallel", "parallel", "arbitrary")),
        name="gdn_conv",
    )(p32, p32, conv_w)


def _gdn_kernel(q_ref, k_ref, v_ref, gate_ref, small_ref, prm_ref, nrm_ref, o_ref, state_ref,
                *, n_chunks):
    c = GDN_CHUNK

    @pl.when(pl.program_id(1) == 0)
    def _():
        state_ref[...] = jnp.zeros_like(state_ref)

    r = lax.broadcasted_iota(I32, (c, c), 0)
    cc = lax.broadcasted_iota(I32, (c, c), 1)
    incl = r >= cc
    strict = r > cc
    tri_incl = jnp.where(incl, 1.0, 0.0).astype(BF16)
    eye = jnp.where(r == cc, 1.0, 0.0)
    a_log = prm_ref[0:1, :]
    dt_bias = prm_ref[1:2, :]

    def chunk(ci, _):
        rs = pl.ds(pl.multiple_of(ci * c, c), c)
        small = small_ref[rs, :]
        beta_all = _sigmoid(small)
        g_all = -jnp.exp(a_log) * _softplus(small + dt_bias)
        gc_all = _dot_left_exact(tri_incl, g_all)
        for h in range(N_HEADS):
            hs = slice(h * HEAD_DIM, (h + 1) * HEAD_DIM)
            beta = beta_all[:, SMALL_B + h:SMALL_B + h + 1]
            g = g_all[:, SMALL_A + h:SMALL_A + h + 1]
            gc = gc_all[:, SMALL_A + h:SMALL_A + h + 1]
            gc_last = gc[c - 1:c, :]
            dsum = _dot_left_exact(tri_incl, jnp.where(strict, g, 0.0))
            decay = jnp.where(incl, jnp.exp(jnp.where(incl, dsum, 0.0)), 0.0)
            q = q_ref[rs, hs] * (HEAD_DIM ** -0.5)
            k = k_ref[rs, hs]
            v = v_ref[rs, hs]
            kb16 = _bf(k)
            k_beta = k * beta
            a = jnp.where(strict, _dot_nt(_bf(k_beta), kb16) * decay, 0.0)
            nk = -a
            inv = eye + nk
            for _i in range(5):
                nk = _dot_hp(nk, nk)
                inv = inv + _dot_hp(inv, nk)
            e_gc = jnp.exp(gc)
            rhs = jnp.concatenate([v * beta, k_beta * e_gc], axis=-1)
            sol = _dot_hp(inv, rhs)
            u = sol[:, :HEAD_DIM]
            w = sol[:, HEAD_DIM:]
            qk = jnp.where(incl, _dot_nt(_bf(q), kb16) * decay, 0.0)
            q_dec = q * e_gc
            k_dec = k * jnp.exp(gc_last - gc)
            state = state_ref[h]
            s16 = _bf(state)
            v_new = u - _dot(_bf(w), s16)
            vn16 = _bf(v_new)
            o = _dot(_bf(q_dec), s16) + _dot(_bf(qk), vn16)
            state_ref[h] = state * jnp.exp(gc_last) + _dot_tn(_bf(k_dec), vn16)
            ms = jnp.mean(o * o, axis=-1, keepdims=True)
            y = o * lax.rsqrt(ms + NORM_EPS) * nrm_ref[...]
            o_ref[rs, hs] = (y * _silu(gate_ref[rs, hs])).astype(o_ref.dtype)
        return 0

    lax.fori_loop(0, n_chunks, chunk, 0)


def _gated_deltanet(gq, p32, prm, gnorm, batch, seq):
    lb = min(256, seq)
    nb = seq // lb
    t = batch * seq
    blk = lambda cb: pl.BlockSpec((lb, BRANCH_W), lambda b, j, cb=cb: (b * nb + j, cb))
    return pl.pallas_call(
        functools.partial(_gdn_kernel, n_chunks=lb // GDN_CHUNK),
        grid=(batch, nb),
        in_specs=[blk(0), blk(1), blk(2), blk(7),
                  pl.BlockSpec((lb, LANES), lambda b, j: (b * nb + j, P32_SMALL_BLK)),
                  pl.BlockSpec((SUBLANES, LANES), lambda b, j: (0, 0)),
                  pl.BlockSpec((1, HEAD_DIM), lambda b, j: (0, 0))],
        out_specs=pl.BlockSpec((lb, BRANCH_W), lambda b, j: (b * nb + j, 0)),
        out_shape=jax.ShapeDtypeStruct((t, BRANCH_W), BF16),
        scratch_shapes=[pltpu.VMEM((N_HEADS, HEAD_DIM, HEAD_DIM), F32)],
        compiler_params=_cparams(("parallel", "arbitrary")),
        name="gated_deltanet",
    )(gq, gq, gq, p32, p32, prm, gnorm)


def _merge_kernel(x_ref, ya_ref, yb_ref, yc_ref, yd_ref, wg_ref, wb_ref, o_ref):
    x = x_ref[...]
    acc = None
    for i, y_ref in enumerate((ya_ref, yb_ref, yc_ref, yd_ref)):
        term = _sigmoid(_dot(x, wg_ref[i])) * _dot(y_ref[...], wb_ref[i])
        acc = term if acc is None else acc + term
    o_ref[...] = acc.astype(o_ref.dtype)


def _merge(x16, ys, wg16, wb16):
    t = x16.shape[0]
    tm = min(1024, t)
    tn = 256
    yspec = pl.BlockSpec((tm, BRANCH_W), lambda i, j: (i, 0))
    return pl.pallas_call(
        _merge_kernel,
        grid=(t // tm, D_MODEL // tn),
        in_specs=[pl.BlockSpec((tm, D_MODEL), lambda i, j: (i, 0)), yspec, yspec, yspec, yspec,
                  pl.BlockSpec((4, D_MODEL, tn), lambda i, j: (0, 0, j)),
                  pl.BlockSpec((4, BRANCH_W, tn), lambda i, j: (0, 0, j))],
        out_specs=pl.BlockSpec((tm, tn), lambda i, j: (i, j)),
        out_shape=jax.ShapeDtypeStruct((t, D_MODEL), BF16),
        compiler_params=_cparams(("parallel", "arbitrary")),
        name="gated_merge",
    )(x16, *ys, wg16, wb16)


def _layer_norm_rows(h, g, b):
    mu = jnp.mean(h, axis=-1, keepdims=True)
    d = h - mu
    var = jnp.mean(d * d, axis=-1, keepdims=True)
    return d * lax.rsqrt(var + NORM_EPS) * g + b


def _outproj_ln_kernel(m_ref, w_ref, x_ref, g_ref, b_ref, o32_ref, o16_ref):
    h = ALPHA * x_ref[...] + _dot(m_ref[...], w_ref[...])
    y = _layer_norm_rows(h, g_ref[...], b_ref[...])
    o32_ref[...] = y
    o16_ref[...] = _bf(y)


def _outproj_ln(merged16, w_out16, x32, g, b):
    t = x32.shape[0]
    tm = min(256, t)
    row = pl.BlockSpec((tm, D_MODEL), lambda i: (i, 0))
    vec = pl.BlockSpec((1, D_MODEL), lambda i: (0, 0))
    return pl.pallas_call(
        _outproj_ln_kernel,
        grid=(t // tm,),
        in_specs=[row, pl.BlockSpec((D_MODEL, D_MODEL), lambda i: (0, 0)), row, vec, vec],
        out_specs=[row, row],
        out_shape=[jax.ShapeDtypeStruct((t, D_MODEL), F32), jax.ShapeDtypeStruct((t, D_MODEL), BF16)],
        compiler_params=_cparams(("parallel",)),
        name="outproj_ln",
    )(merged16, w_out16, x32, g, b)


def _router_kernel(x_ref, whi_ref, wlo_ref, b_ref, ids_ref, gates_ref):
    xh, xl = _split2(x_ref[...])
    whi = whi_ref[...]
    logits = _dot(xh, whi) + _dot(xh, wlo_ref[...]) + _dot(xl, whi) + b_ref[...]
    tm = logits.shape[0]
    lane = lax.broadcasted_iota(I32, (tm, LANES), 1)
    ninf = -jnp.inf
    gl = jnp.where(lane < N_GROUPS, logits, ninf)
    gmax = jnp.max(gl, axis=-1, keepdims=True)
    grp = jnp.min(jnp.where(gl == gmax, lane, LANES), axis=-1, keepdims=True)
    p_grp = 1.0 / jnp.sum(jnp.where(lane < N_GROUPS, jnp.exp(logits - gmax), 0.0), axis=-1, keepdims=True)
    lo = N_GROUPS + grp * EXPERTS_PER_GROUP
    el = jnp.where((lane >= lo) & (lane < lo + EXPERTS_PER_GROUP), logits, ninf)
    v1 = jnp.max(el, axis=-1, keepdims=True)
    i1 = jnp.min(jnp.where(el == v1, lane, LANES), axis=-1, keepdims=True)
    el2 = jnp.where(lane == i1, ninf, el)
    v2 = jnp.max(el2, axis=-1, keepdims=True)
    i2 = jnp.min(jnp.where(el2 == v2, lane, LANES), axis=-1, keepdims=True)
    e21 = jnp.exp(v2 - v1)
    den = 1.0 / (1.0 + e21)
    g1 = p_grp * den
    g2 = p_grp * (e21 * den)
    ids_ref[...] = jnp.where(lane == 0, i1 - N_GROUPS, jnp.where(lane == 1, i2 - N_GROUPS, 0))
    gates_ref[...] = jnp.where(lane == 0, g1, jnp.where(lane == 1, g2, 0.0))


def _router(x32, whi, wlo, bias):
    t = x32.shape[0]
    tm = min(512, t)
    row = pl.BlockSpec((tm, LANES), lambda i: (i, 0))
    wsp = pl.BlockSpec((D_MODEL, LANES), lambda i: (0, 0))
    return pl.pallas_call(
        _router_kernel,
        grid=(t // tm,),
        in_specs=[pl.BlockSpec((tm, D_MODEL), lambda i: (i, 0)), wsp, wsp,
                  pl.BlockSpec((1, LANES), lambda i: (0, 0))],
        out_specs=[row, row],
        out_shape=[jax.ShapeDtypeStruct((t, LANES), I32), jax.ShapeDtypeStruct((t, LANES), F32)],
        compiler_params=_cparams(("parallel",)),
        name="moe_router",
    )(x32, whi, wlo, bias)


def _expert_kernel(be_ref, nb_ref, tok_ref, x_hbm, wg_ref, wu_ref, wd_ref, o_ref, xbuf, sem, *, blk):
    i = pl.program_id(0)

    def row_copy(r, tok):
        return pltpu.make_async_copy(x_hbm.at[pl.ds(tok, 1), :], xbuf.at[pl.ds(r, 1), :], sem.at[0])

    @pl.when(i < nb_ref[0])
    def _():
        def start(r, _):
            row_copy(r, tok_ref[0, 0, r]).start()
            return 0

        lax.fori_loop(0, blk, start, 0)

        def wait(r, _):
            row_copy(r, 0).wait()
            return 0

        lax.fori_loop(0, blk, wait, 0)
        xe = _bf(xbuf[...])
        h = _silu(_dot(xe, wg_ref[...])) * _dot(xe, wu_ref[...])
        o_ref[...] = _dot(_bf(h), wd_ref[...])

    @pl.when(i >= nb_ref[0])
    def _():
        o_ref[...] = jnp.zeros_like(o_ref)


def _experts(x32, slot_tok, block_e, nb_used, wg16, wu16, wd16, blk):
    n_blocks = block_e.shape[0]
    wspec = lambda shape: pl.BlockSpec((None,) + shape, lambda i, be, nb: (be[i], 0, 0))
    grid_spec = pltpu.PrefetchScalarGridSpec(
        num_scalar_prefetch=2,
        grid=(n_blocks,),
        in_specs=[pl.BlockSpec((1, 1, blk), lambda i, be, nb: (i, 0, 0), memory_space=pltpu.SMEM),
                  pl.BlockSpec(memory_space=pl.ANY),
                  wspec((D_MODEL, D_EXPERT)), wspec((D_MODEL, D_EXPERT)), wspec((D_EXPERT, D_MODEL))],
        out_specs=pl.BlockSpec((blk, D_MODEL), lambda i, be, nb: (i, 0)),
        scratch_shapes=[pltpu.VMEM((blk, D_MODEL), F32), pltpu.SemaphoreType.DMA((1,))],
    )
    return pl.pallas_call(
        functools.partial(_expert_kernel, blk=blk),
        grid_spec=grid_spec,
        out_shape=jax.ShapeDtypeStruct((n_blocks * blk, D_MODEL), F32),
        compiler_params=_cparams(("arbitrary",)),
        name="moe_experts",
    )(block_e, nb_used, slot_tok.reshape(n_blocks, 1, blk), x32, wg16, wu16, wd16)


def _combine_kernel(slot_ref, y_hbm, x_ref, gates_ref, g_ref, b_ref, o32_ref, o16_ref, ybuf, sem, *, tm):
    def row_copy(r, k, slot):
        return pltpu.make_async_copy(y_hbm.at[pl.ds(slot, 1), :], ybuf.at[k, pl.ds(r, 1), :], sem.at[0])

    def start(r, _):
        row_copy(r, 0, slot_ref[0, 0, 2 * r]).start()
        row_copy(r, 1, slot_ref[0, 0, 2 * r + 1]).start()
        return 0

    lax.fori_loop(0, tm, start, 0)

    def wait(r, _):
        row_copy(r, 0, 0).wait()
        row_copy(r, 1, 0).wait()
        return 0

    lax.fori_loop(0, tm, wait, 0)
    gates = gates_ref[...]
    y = ybuf[0] * gates[:, 0:1] + ybuf[1] * gates[:, 1:2]
    out = _layer_norm_rows(ALPHA * x_ref[...] + y, g_ref[...], b_ref[...])
    o32_ref[...] = out
    o16_ref[...] = _bf(out)


def _combine_ln(slots, yb, x32, gates, g, b):
    t = x32.shape[0]
    tm = min(128, t)
    row = pl.BlockSpec((tm, D_MODEL), lambda i: (i, 0))
    vec = pl.BlockSpec((1, D_MODEL), lambda i: (0, 0))
    return pl.pallas_call(
        functools.partial(_combine_kernel, tm=tm),
        grid=(t // tm,),
        in_specs=[pl.BlockSpec((1, 1, 2 * tm), lambda i: (i, 0, 0), memory_space=pltpu.SMEM),
                  pl.BlockSpec(memory_space=pl.ANY), row,
                  pl.BlockSpec((tm, LANES), lambda i: (i, 0)), vec, vec],
        out_specs=[row, row],
        out_shape=[jax.ShapeDtypeStruct((t, D_MODEL), F32), jax.ShapeDtypeStruct((t, D_MODEL), BF16)],
        scratch_shapes=[pltpu.VMEM((2, tm, D_MODEL), F32), pltpu.SemaphoreType.DMA((1,))],
        compiler_params=_cparams(("arbitrary",)),
        name="moe_combine_ln",
    )(slots.reshape(t // tm, 1, 2 * tm), yb, x32, gates, g, b)


def _moe_plan(ids, blk):
    t = ids.shape[0]
    n_assign = 2 * t
    e_flat = ids[:, :2].reshape(-1)
    onehot = (e_flat[:, None] == jnp.arange(N_EXPERTS, dtype=I32)[None, :]).astype(I32)
    rank = jnp.sum((jnp.cumsum(onehot, axis=0) - onehot) * onehot, axis=-1)
    sizes = jnp.sum(onehot, axis=0)
    padded = (sizes + blk - 1) // blk * blk
    pad_ends = jnp.cumsum(padded)
    pad_starts = pad_ends - padded
    slot = (pad_starts[e_flat] + rank).astype(I32)
    n_blocks = n_assign // blk + N_EXPERTS
    slot_tok = jnp.zeros((n_blocks * blk,), I32).at[slot].set(jnp.arange(n_assign, dtype=I32) // 2)
    block_e = jnp.minimum(jnp.searchsorted(pad_ends, jnp.arange(n_blocks, dtype=I32) * blk, side='right'),
                          N_EXPERTS - 1).astype(I32)
    nb_used = (pad_ends[-1:] // blk).astype(I32)
    return slot, slot_tok, block_e, nb_used


MOE_BLK = 128
IN_TM, IN_TN = 1024, 512


def _prep_weights(w_in, dsa_q_norm, w_uq, gdn_conv, gdn_a_log, gdn_dt_bias, gdn_norm, w_branch,
                  w_branch_gate, w_out, ln1_g, ln1_b, w_router_group, b_router_group,
                  w_router_expert, b_router_expert, w_exp_gate, w_exp_up, w_exp_down, ln2_g, ln2_b):
    depth = w_in.shape[0]
    zc = lambda n: jnp.zeros((depth, D_MODEL, n), F32)
    c = lambda a, b: w_in[..., a:b]
    w16 = jnp.concatenate([c(0, 1536), c(3968, 4224), c(4224, 4288), zc(P16_W - 1856)], -1).astype(BF16)
    w32 = jnp.concatenate([c(4296, 5832), c(1536, 3584), c(5840, 6352), c(3584, 3968),
                           c(4288, 4296), c(5832, 5836), c(5836, 5840), zc(LANES - 16)], -1).astype(BF16)
    qi = w_uq[..., BRANCH_W:].reshape(depth, DSA_Q_RANK, IDX_HEADS, IDX_DIM)
    qi = jnp.pad(qi, ((0, 0), (0, 0), (0, 0), (0, LANES - IDX_DIM))).reshape(depth, DSA_Q_RANK, IDX_HEADS * LANES)
    wuq = jnp.concatenate([qi, w_uq[..., :BRANCH_W]], -1).astype(BF16)
    wr = jnp.concatenate([w_router_group, w_router_expert,
                          zc(LANES - N_GROUPS - N_EXPERTS)], -1)
    rhi = wr.astype(BF16)
    rlo = (wr - rhi.astype(F32)).astype(BF16)
    rb = jnp.concatenate([b_router_group, b_router_expert,
                          jnp.zeros((depth, LANES - N_GROUPS - N_EXPERTS), F32)], -1)[:, None, :]
    prm = jnp.zeros((depth, SUBLANES, LANES), F32)
    prm = prm.at[:, 0, SMALL_A:SMALL_A + N_HEADS].set(gdn_a_log)
    prm = prm.at[:, 1, SMALL_A:SMALL_A + N_HEADS].set(gdn_dt_bias)
    return dict(
        w16=w16, w32=w32, q_norm=dsa_q_norm[:, None, :], wuq=wuq, conv=gdn_conv, prm=prm,
        gnorm=gdn_norm[:, None, :], wb=w_branch.astype(BF16), wg=w_branch_gate.astype(BF16),
        wout=w_out.astype(BF16), ln1g=ln1_g[:, None, :], ln1b=ln1_b[:, None, :], rhi=rhi, rlo=rlo, rb=rb,
        eg=w_exp_gate.astype(BF16), eu=w_exp_up.astype(BF16), ed=w_exp_down.astype(BF16),
        ln2g=ln2_g[:, None, :], ln2b=ln2_b[:, None, :])


def _mixer(x16, lw, batch, seq):
    p16 = _matmul(x16, lw['w16'], BF16, IN_TM, IN_TN, "in_proj_bf16")
    p32 = _matmul(x16, lw['w32'], F32, IN_TM, IN_TN, "in_proj_f32")
    y_a = _sb_attention(p16, batch, seq)
    y_b = _retention(p32, batch, seq)
    q_all = _dsa_query(p32, lw['q_norm'], lw['wuq'])
    y_c = _dsa_attention(q_all, p32, p16, batch, seq)
    gq = _gdn_conv(p32, lw['conv'], batch, seq)
    y_d = _gated_deltanet(gq, p32, lw['prm'], lw['gnorm'], batch, seq)
    return _merge(x16, (y_a, y_b, y_c, y_d), lw['wg'], lw['wb'])


def _moe(x32, lw):
    ids, gates = _router(x32, lw['rhi'], lw['rlo'], lw['rb'])
    slot, slot_tok, block_e, nb_used = _moe_plan(ids, MOE_BLK)
    yb = _experts(x32, slot_tok, block_e, nb_used, lw['eg'], lw['eu'], lw['ed'], MOE_BLK)
    return _combine_ln(slot, yb, x32, gates, lw['ln2g'], lw['ln2b'])


def _layer(x32, x16, lw, batch, seq):
    merged = _mixer(x16, lw, batch, seq)
    x32, x16 = _outproj_ln(merged, lw['wout'], x32, lw['ln1g'], lw['ln1b'])
    return _moe(x32, lw)


def kernel(x, w_in, dsa_q_norm, w_uq, gdn_conv, gdn_a_log, gdn_dt_bias, gdn_norm, w_branch, w_branch_gate,
           w_out, ln1_g, ln1_b, w_router_group, b_router_group, w_router_expert, b_router_expert,
           w_exp_gate, w_exp_up, w_exp_down, ln2_g, ln2_b):
    batch, seq, d = x.shape
    assert d == D_MODEL and seq % RET_CHUNK == 0
    ws = _prep_weights(w_in, dsa_q_norm, w_uq, gdn_conv, gdn_a_log, gdn_dt_bias, gdn_norm, w_branch,
                       w_branch_gate, w_out, ln1_g, ln1_b, w_router_group, b_router_group,
                       w_router_expert, b_router_expert, w_exp_gate, w_exp_up, w_exp_down, ln2_g, ln2_b)
    x32 = x.reshape(batch * seq, d)
    x16 = x32.astype(BF16)
    for l in range(w_in.shape[0]):
        lw = {k: v[l] for k, v in ws.items()}
        x32, x16 = _layer(x32, x16, lw, batch, seq)
    return x32.reshape(batch, seq, d)
```

```python
import functools
import math

import numpy as np
import jax
import jax.numpy as jnp
from jax import lax
from jax.experimental import pallas as pl
from jax.experimental.pallas import tpu as pltpu

F32 = jnp.float32
BF16 = jnp.bfloat16
I32 = jnp.int32

D_MODEL = 2048
DEPTH = 4
HEAD_DIM = 128
N_HEADS = 4
BRANCH_W = N_HEADS * HEAD_DIM
RET_CHUNK = 128
DSA_Q_RANK = 384
DSA_TOPK = 256
IDX_HEADS = 8
IDX_DIM = 64
GDN_CONV = 4
GDN_CHUNK = 64
N_GROUPS = 4
EXPERTS_PER_GROUP = 8
N_EXPERTS = N_GROUPS * EXPERTS_PER_GROUP
D_EXPERT = 512
ALPHA = (2.0 * DEPTH) ** 0.25
NORM_EPS = 1e-5

LANES = 128
SUBLANES = 8
VMEM_LIMIT_BYTES = 56 * 1024 * 1024

P16_W = 2048
P16_DSA_K_BLK = 12
P16_DSA_V_BLK = 13
P16_IDXK_BLK = 14
P32_W = 4608
P32_SMALL_BLK = 35
SMALL_IDXW, SMALL_B, SMALL_A = 0, 8, 12
UQ_W = IDX_HEADS * LANES + BRANCH_W
UQ_Q_BLK = IDX_HEADS * LANES // BRANCH_W

INT_MIN = -2147483648


def _cparams(sem):
    return pltpu.CompilerParams(dimension_semantics=sem, vmem_limit_bytes=VMEM_LIMIT_BYTES)


def _bf(x):
    return x.astype(BF16)


def _dot(a, b):
    return jnp.dot(a, b, preferred_element_type=F32)


def _dot_nt(a, b):
    return lax.dot_general(a, b, (((1,), (1,)), ((), ())), preferred_element_type=F32)


def _dot_tn(a, b):
    return lax.dot_general(a, b, (((0,), (0,)), ((), ())), preferred_element_type=F32)


def _split2(x):
    hi = _bf(x)
    lo = _bf(x - hi.astype(F32))
    return hi, lo


def _split3(x):
    hi = _bf(x)
    r = x - hi.astype(F32)
    mid = _bf(r)
    lo = _bf(r - mid.astype(F32))
    return hi, mid, lo


def _dot_left_exact(m01, x):
    hi, mid, lo = _split3(x)
    return _dot(m01, hi) + _dot(m01, mid) + _dot(m01, lo)


def _dot_right_exact(x, m01):
    hi, mid, lo = _split3(x)
    return _dot(hi, m01) + _dot(mid, m01) + _dot(lo, m01)


def _dot_hp(a, b):
    ah, al = _split2(a)
    bh, bl = _split2(b)
    return _dot(ah, bh) + _dot(ah, bl) + _dot(al, bh)


def _silu(x):
    return x * (1.0 / (1.0 + jnp.exp(-x)))


def _sigmoid(x):
    return 1.0 / (1.0 + jnp.exp(-x))


def _softplus(x):
    return jnp.maximum(x, 0.0) + jnp.log1p(jnp.exp(-jnp.abs(x)))


def _mm_kernel(x_ref, w_ref, o_ref):
    o_ref[...] = _dot(x_ref[...], w_ref[...]).astype(o_ref.dtype)


def _matmul(x, wl, out_dtype, tm, tn, name):
    w, l = wl
    m, k = x.shape
    n = w.shape[2]
    tm = min(tm, m)
    return pl.pallas_call(
        _mm_kernel,
        grid=(m // tm, n // tn),
        in_specs=[pl.BlockSpec((tm, k), lambda i, j: (i, 0)),
                  pl.BlockSpec((None, k, tn), lambda i, j: (l, 0, j))],
        out_specs=pl.BlockSpec((tm, tn), lambda i, j: (i, j)),
        out_shape=jax.ShapeDtypeStruct((m, n), out_dtype),
        compiler_params=_cparams(("parallel", "arbitrary")),
        name=name,
    )(x, w)


def _sb_kernel(q_ref, k_ref, v_ref, o_ref, *, tq):
    i = pl.program_id(1)
    scale = HEAD_DIM ** -0.5
    hsl = lambda h: slice(h * HEAD_DIM, (h + 1) * HEAD_DIM)
    qs = [q_ref[:, hsl(h)] for h in range(N_HEADS)]
    row = i * tq + lax.broadcasted_iota(I32, (tq, 1), 0)
    col0 = lax.broadcasted_iota(I32, (1, tq), 1)
    jr = lax.broadcasted_iota(I32, (tq, tq), 0)
    jc = lax.broadcasted_iota(I32, (tq, tq), 1)
    later = jnp.where(jr > jc, 1.0, 0.0).astype(BF16)

    def body(jj, carry):
        kb = i - jj
        ks = pl.multiple_of(kb * tq, tq)
        causal = (kb * tq + col0) < row
        out = []
        for h in range(N_HEADS):
            acc, run = carry[h]
            kblk = k_ref[pl.ds(ks, tq), hsl(h)]
            vblk = v_ref[pl.ds(ks, tq), hsl(h)]
            z = _dot_nt(qs[h], kblk) * scale
            sp = jnp.log(1.0 + jnp.exp(-jnp.abs(z)))
            log_beta = jnp.minimum(z, 0.0) - sp
            log_1m = jnp.where(causal, log_beta - z, 0.0)
            hi, lo = _split2(log_1m)
            suffix = _dot(hi, later) + _dot(lo, later)
            w = jnp.where(causal, jnp.exp(log_beta + suffix + run), 0.0)
            out.append((acc + _dot(_bf(w), vblk), run + suffix[:, 0:1] + log_1m[:, 0:1]))
        return tuple(out)

    init = tuple((jnp.zeros((tq, HEAD_DIM), F32), jnp.zeros((tq, 1), F32)) for _ in range(N_HEADS))
    res = lax.fori_loop(0, i + 1, body, init)
    for h in range(N_HEADS):
        o_ref[:, hsl(h)] = res[h][0].astype(o_ref.dtype)


def _sb_attention(p16, batch, seq):
    tq = min(256, seq)
    nq = seq // tq
    t = batch * seq
    return pl.pallas_call(
        functools.partial(_sb_kernel, tq=tq),
        grid=(batch, nq),
        in_specs=[pl.BlockSpec((tq, BRANCH_W), lambda b, i: (b * nq + i, 0)),
                  pl.BlockSpec((seq, BRANCH_W), lambda b, i: (b, 1)),
                  pl.BlockSpec((seq, BRANCH_W), lambda b, i: (b, 2))],
        out_specs=pl.BlockSpec((tq, BRANCH_W), lambda b, i: (b * nq + i, 0)),
        out_shape=jax.ShapeDtypeStruct((t, BRANCH_W), BF16),
        compiler_params=_cparams(("parallel", "arbitrary")),
        name="sb_attention",
    )(p16, p16, p16)


def _ret_kernel(dch_ref, q_ref, k_ref, v_ref, g_ref, cos_ref, sin_ref, dintra_ref, dq_ref, dk_ref,
                o_ref, state_ref, *, n_chunks):
    c = RET_CHUNK

    @pl.when(pl.program_id(1) == 0)
    def _():
        state_ref[...] = jnp.zeros_like(state_ref)

    def rot(x, cs, sn):
        return x * cs + pltpu.roll(x, HEAD_DIM // 2, 1) * sn

    for h in range(N_HEADS):
        hs = slice(h * HEAD_DIM, (h + 1) * HEAD_DIM)
        state = state_ref[h]
        for ci in range(n_chunks):
            rs = slice(ci * c, (ci + 1) * c)
            cs = cos_ref[rs, :]
            sn = sin_ref[rs, :]
            qc = rot(q_ref[rs, hs], cs, sn)
            kc = rot(k_ref[rs, hs], cs, sn) * (HEAD_DIM ** -0.5)
            vc = _bf(v_ref[rs, hs])
            qb = _bf(qc)
            s = _dot_nt(qb, _bf(kc)) * dintra_ref[h]
            o = _dot(_bf(s), vc) + _dot(qb, _bf(state)) * dq_ref[h]
            kv = _dot_tn(_bf(kc * dk_ref[h]), vc)
            state = dch_ref[h] * state + kv
            mu = jnp.mean(o, axis=-1, keepdims=True)
            d = o - mu
            var = jnp.mean(d * d, axis=-1, keepdims=True)
            y = d * lax.rsqrt(var + NORM_EPS)
            o_ref[rs, hs] = (y * _silu(g_ref[rs, hs])).astype(o_ref.dtype)
        state_ref[h] = state


def _retention(p32, batch, seq):
    lb = min(512, seq)
    nb = seq // lb
    t = batch * seq
    f32 = F32
    hh = N_HEADS
    c = RET_CHUNK
    log_gamma = jnp.log1p(-jnp.exp2(-5.0 - jnp.arange(hh, dtype=f32)))
    pos = jnp.arange(c, dtype=f32)
    rel = pos[:, None] - pos[None, :]
    causal = rel >= 0
    d_intra = jnp.where(causal, jnp.exp(jnp.where(causal, rel, 0.0) * log_gamma[:, None, None]), 0.0)
    d_q = jnp.exp((pos + 1.0) * log_gamma[:, None])
    d_k = jnp.exp((c - 1.0 - pos) * log_gamma[:, None])
    d_chunk = jnp.exp(c * log_gamma)
    dq_b = jnp.broadcast_to(d_q[:, :, None], (hh, c, HEAD_DIM))
    dk_b = jnp.broadcast_to(d_k[:, :, None], (hh, c, HEAD_DIM))
    inv = 1.0 / (10000.0 ** (jnp.arange(0, HEAD_DIM, 2, dtype=f32) / HEAD_DIM))
    ang = jnp.arange(seq).astype(f32)[:, None] * inv[None, :]
    cos_t = jnp.concatenate([jnp.cos(ang), jnp.cos(ang)], -1)
    sin_t = jnp.concatenate([-jnp.sin(ang), jnp.sin(ang)], -1)

    qkvg = lambda blk: pl.BlockSpec((lb, BRANCH_W), lambda b, j, blk=blk: (b * nb + j, blk))
    tab = pl.BlockSpec((lb, HEAD_DIM), lambda b, j: (j, 0))
    whole3 = pl.BlockSpec((hh, c, HEAD_DIM), lambda b, j: (0, 0, 0))
    return pl.pallas_call(
        functools.partial(_ret_kernel, n_chunks=lb // c),
        grid=(batch, nb),
        in_specs=[pl.BlockSpec(memory_space=pltpu.SMEM),
                  qkvg(3), qkvg(4), qkvg(5), qkvg(6), tab, tab, whole3, whole3, whole3],
        out_specs=pl.BlockSpec((lb, BRANCH_W), lambda b, j: (b * nb + j, 0)),
        out_shape=jax.ShapeDtypeStruct((t, BRANCH_W), BF16),
        scratch_shapes=[pltpu.VMEM((hh, HEAD_DIM, HEAD_DIM), F32)],
        compiler_params=_cparams(("parallel", "arbitrary")),
        name="retention",
    )(d_chunk, p32, p32, p32, p32, cos_t, sin_t, d_intra, dq_b, dk_b)


def _uq_kernel(x_ref, g_ref, w_ref, o_ref):
    x = x_ref[:, :DSA_Q_RANK]
    ms = jnp.mean(x * x, axis=-1, keepdims=True)
    xn = x * lax.rsqrt(ms + NORM_EPS) * g_ref[...]
    o_ref[...] = _dot(_bf(xn), w_ref[...]).astype(o_ref.dtype)


def _dsa_query(p32, q_norm, w_uq16):
    t = p32.shape[0]
    tm = min(512, t)
    return pl.pallas_call(
        _uq_kernel,
        grid=(t // tm,),
        in_specs=[pl.BlockSpec((tm, 512), lambda i: (i, 8)),
                  pl.BlockSpec((1, DSA_Q_RANK), lambda i: (0, 0)),
                  pl.BlockSpec((DSA_Q_RANK, UQ_W), lambda i: (0, 0))],
        out_specs=pl.BlockSpec((tm, UQ_W), lambda i: (i, 0)),
        out_shape=jax.ShapeDtypeStruct((t, UQ_W), BF16),
        compiler_params=_cparams(("parallel",)),
        name="dsa_query",
    )(p32, q_norm, w_uq16)


def _dsa_kernel(q_ref, qi_ref, small_ref, kidx_ref, k_ref, v_ref, o_ref, keys_ref,
                *, tq, kc, topk):
    i = pl.program_id(1)
    nkv = ((i + 1) * tq + kc - 1) // kc
    row = i * tq + lax.broadcasted_iota(I32, (tq, 1), 0)
    col0 = lax.broadcasted_iota(I32, (1, kc), 1)
    wrow = small_ref[...] * (IDX_HEADS ** -0.5) * (IDX_DIM ** -0.5)
    wcols = [wrow[:, SMALL_IDXW + h:SMALL_IDXW + h + 1] for h in range(IDX_HEADS)]
    qis = [qi_ref[:, h * LANES:(h + 1) * LANES] for h in range(IDX_HEADS)]

    def score_chunk(j, _):
        ks = pl.multiple_of(j * kc, kc)
        kix = kidx_ref[pl.ds(ks, kc), :]
        acc = jnp.zeros((tq, kc), F32)
        for h in range(IDX_HEADS):
            acc = acc + jnp.maximum(_dot_nt(qis[h], kix), 0.0) * wcols[h]
        bits = pltpu.bitcast(acc, I32)
        key = jnp.where(bits < 0, bits ^ 0x7FFFFFFF, bits)
        causal = (j * kc + col0) <= row
        keys_ref[j] = jnp.where(causal, key, INT_MIN)
        return 0

    lax.fori_loop(0, nkv, score_chunk, 0)

    nsub = kc // LANES
    ones_l = jnp.ones((LANES, LANES), BF16)

    def count_ge(cand):
        def cbody(j, part):
            kj = keys_ref[j]
            for s in range(nsub):
                part = part + jnp.where(kj[:, s * LANES:(s + 1) * LANES] >= cand, 1.0, 0.0)
            return part
        part = lax.fori_loop(0, nkv, cbody, jnp.zeros((tq, LANES), F32))
        return _dot(_bf(part), ones_l)

    def bit_step(it, theta):
        cand = theta + lax.shift_left(jnp.int32(1), 31 - it)
        return jnp.where(count_ge(cand) >= topk, cand, theta)

    theta = lax.fori_loop(0, 32, bit_step, jnp.full((tq, LANES), INT_MIN, I32))
    n_gt = jnp.where(theta == 2147483647, 0.0, count_ge(theta + 1))
    need = jnp.where(theta == INT_MIN, 0.0, topk - n_gt)
    theta_w = jnp.concatenate([theta] * nsub, axis=1)
    need_w = jnp.concatenate([need] * nsub, axis=1)

    ur = lax.broadcasted_iota(I32, (kc, kc), 0)
    uc = lax.broadcasted_iota(I32, (kc, kc), 1)
    upto = jnp.where(ur <= uc, 1.0, 0.0).astype(BF16)
    qs = [q_ref[:, h * HEAD_DIM:(h + 1) * HEAD_DIM] for h in range(N_HEADS)]
    scale = HEAD_DIM ** -0.5
    ninf = -jnp.inf

    def attend(j, carry):
        seen, ms, ls, accs = carry
        ks = pl.multiple_of(j * kc, kc)
        key = keys_ref[j]
        eq = key == theta_w
        pc = _dot(jnp.where(eq, 1.0, 0.0).astype(BF16), upto) + seen
        rank = jnp.where(eq, pc - need_w, jnp.where(key > theta_w, -1.0, 1.0))
        sel = rank <= 0.0
        seen = pc[:, kc - 1:kc]
        kblk = k_ref[pl.ds(ks, kc), :]
        vblk = v_ref[pl.ds(ks, kc), :]
        ms2, ls2, accs2 = [], [], []
        for h in range(N_HEADS):
            logit = jnp.where(sel, _dot_nt(qs[h], kblk) * scale, ninf)
            m_new = jnp.maximum(ms[h], jnp.max(logit, axis=-1, keepdims=True))
            p = jnp.exp(logit - m_new)
            a = jnp.exp(ms[h] - m_new)
            ls2.append(a * ls[h] + jnp.sum(p, axis=-1, keepdims=True))
            accs2.append(a * accs[h] + _dot(_bf(p), vblk))
            ms2.append(m_new)
        return seen, tuple(ms2), tuple(ls2), tuple(accs2)

    init = (jnp.zeros((tq, 1), F32),
            tuple(jnp.full((tq, 1), -1e30, F32) for _ in range(N_HEADS)),
            tuple(jnp.zeros((tq, 1), F32) for _ in range(N_HEADS)),
            tuple(jnp.zeros((tq, HEAD_DIM), F32) for _ in range(N_HEADS)))
    _, _, ls, accs = lax.fori_loop(0, nkv, attend, init)
    for h in range(N_HEADS):
        o_ref[:, h * HEAD_DIM:(h + 1) * HEAD_DIM] = (accs[h] / ls[h]).astype(o_ref.dtype)


def _dsa_attention(q_all, p32, p16, batch, seq):
    tq = min(256, seq)
    kc = min(512, seq)
    nq = seq // tq
    t = batch * seq
    topk = min(DSA_TOPK, seq // 4)
    kv = lambda blk: pl.BlockSpec((seq, LANES), lambda b, i, blk=blk: (b, blk))
    return pl.pallas_call(
        functools.partial(_dsa_kernel, tq=tq, kc=kc, topk=topk),
        grid=(batch, nq),
        in_specs=[pl.BlockSpec((tq, BRANCH_W), lambda b, i: (b * nq + i, UQ_Q_BLK)),
                  pl.BlockSpec((tq, IDX_HEADS * LANES), lambda b, i: (b * nq + i, 0)),
                  pl.BlockSpec((tq, LANES), lambda b, i: (b * nq + i, P32_SMALL_BLK)),
                  kv(P16_IDXK_BLK), kv(P16_DSA_K_BLK), kv(P16_DSA_V_BLK)],
        out_specs=pl.BlockSpec((tq, BRANCH_W), lambda b, i: (b * nq + i, 0)),
        out_shape=jax.ShapeDtypeStruct((t, BRANCH_W), BF16),
        scratch_shapes=[pltpu.VMEM((seq // kc, tq, kc), I32)],
        compiler_params=_cparams(("parallel", "arbitrary")),
        name="dsa_attention",
    )(q_all, q_all, p32, p16, p16, p16)


def _conv_kernel(prev_ref, x_ref, w_ref, o_ref, *, tl):
    first = pl.program_id(1) == 0
    prev = jnp.where(first, 0.0, prev_ref[...])
    ext = jnp.concatenate([prev, x_ref[...]], axis=0)
    acc = ext[SUBLANES:, :] * w_ref[GDN_CONV - 1:GDN_CONV, :]
    for d in range(1, GDN_CONV):
        acc = acc + pltpu.roll(ext, d, 0)[SUBLANES:, :] * w_ref[GDN_CONV - 1 - d:GDN_CONV - d, :]
    y = _silu(acc)
    is_v = pl.program_id(2) == 2
    for h in range(N_HEADS):
        hs = slice(h * HEAD_DIM, (h + 1) * HEAD_DIM)
        yh = y[:, hs]
        nrm = lax.rsqrt(jnp.sum(yh * yh, axis=-1, keepdims=True) + 1e-6)
        o_ref[:, hs] = yh * jnp.where(is_v, 1.0, nrm)


def _gdn_conv(p32, conv_w, batch, seq):
    tl = min(256, seq)
    nl = seq // tl
    t = batch * seq
    per8 = tl // SUBLANES
    return pl.pallas_call(
        functools.partial(_conv_kernel, tl=tl),
        grid=(batch, nl, 3),
        in_specs=[pl.BlockSpec((SUBLANES, BRANCH_W),
                               lambda b, i, c: (jnp.maximum((b * nl + i) * per8 - 1, 0), c)),
                  pl.BlockSpec((tl, BRANCH_W), lambda b, i, c: (b * nl + i, c)),
                  pl.BlockSpec((GDN_CONV, BRANCH_W), lambda b, i, c: (0, c))],
        out_specs=pl.BlockSpec((tl, BRANCH_W), lambda b, i, c: (b * nl + i, c)),
        out_shape=jax.ShapeDtypeStruct((t, 3 * BRANCH_W), F32),
        compiler_params=_cparams(("parallel", "parallel", "arbitrary")),
        name="gdn_conv",
    )(p32, p32, conv_w)


def _gdn_kernel(q_ref, k_ref, v_ref, gate_ref, small_ref, prm_ref, nrm_ref, o_ref, state_ref,
                *, n_chunks):
    c = GDN_CHUNK

    @pl.when(pl.program_id(1) == 0)
    def _():
        state_ref[...] = jnp.zeros_like(state_ref)

    lb = n_chunks * c
    g4 = N_HEADS * c
    ri = lax.broadcasted_iota(I32, (g4, g4), 0)
    cj = lax.broadcasted_iota(I32, (g4, g4), 1)
    sh = c.bit_length() - 1
    same = (ri >> sh) == (cj >> sh)
    incl = same & (ri >= cj)
    strict = same & (ri > cj)
    eye = jnp.where(ri == cj, 1.0, 0.0)
    li = lax.broadcasted_iota(I32, (lb, lb), 0)
    lj = lax.broadcasted_iota(I32, (lb, lb), 1)
    tri_chunks = jnp.where(((li >> sh) == (lj >> sh)) & (li >= lj), 1.0, 0.0).astype(BF16)

    small = small_ref[...]
    beta_all = _sigmoid(small)
    g_all = -jnp.exp(prm_ref[0:1, :]) * _softplus(small + prm_ref[1:2, :])
    gc_all = _dot_left_exact(tri_chunks, g_all)

    def stack(fn):
        return jnp.concatenate([fn(h) for h in range(N_HEADS)], axis=0)

    hsl = lambda h: slice(h * HEAD_DIM, (h + 1) * HEAD_DIM)
    scale = HEAD_DIM ** -0.5
    prepped = []
    for ci in range(n_chunks):
        rs = slice(ci * c, (ci + 1) * c)
        last = slice((ci + 1) * c - 1, (ci + 1) * c)
        k = stack(lambda h: k_ref[rs, hsl(h)])
        q = stack(lambda h: q_ref[rs, hsl(h)]) * scale
        v = stack(lambda h: v_ref[rs, hsl(h)])
        beta = stack(lambda h: beta_all[rs, SMALL_B + h:SMALL_B + h + 1])
        gc = stack(lambda h: gc_all[rs, SMALL_A + h:SMALL_A + h + 1])
        gc_last = stack(lambda h: jnp.broadcast_to(gc_all[last, SMALL_A + h:SMALL_A + h + 1], (c, 1)))
        gc_row = jnp.transpose(jnp.broadcast_to(gc, (g4, LANES)))[0:1, :]
        decay = jnp.where(incl, jnp.exp(jnp.where(incl, gc - gc_row, 0.0)), 0.0)
        k16 = _bf(k)
        k_beta = k * beta
        a = jnp.where(strict, _dot_nt(_bf(k_beta), k16) * decay, 0.0)
        nk = -a
        inv = eye + nk
        for _i in range(5):
            nk16 = _bf(nk)
            nk = _dot(nk16, nk16)
            inv = inv + _dot(_bf(inv), _bf(nk))
        e_gc = jnp.exp(gc)
        sol = _dot_hp(inv, jnp.concatenate([v * beta, k_beta * e_gc], axis=-1))
        qk = jnp.where(incl, _dot_nt(_bf(q), k16) * decay, 0.0)
        prepped.append(dict(u=sol[:, :HEAD_DIM], w16=_bf(sol[:, HEAD_DIM:]), qk16=_bf(qk),
                            qd16=_bf(q * e_gc), kd16=_bf(k * jnp.exp(gc_last - gc)),
                            cdec=[jnp.exp(gc_all[last, SMALL_A + h:SMALL_A + h + 1]) for h in range(N_HEADS)]))

    states = [state_ref[h] for h in range(N_HEADS)]
    for ci, p in enumerate(prepped):
        rs = slice(ci * c, (ci + 1) * c)
        row = lambda h: slice(h * c, (h + 1) * c)
        s16 = [_bf(s) for s in states]
        vn16 = _bf(stack(lambda h: p['u'][row(h)] - _dot(p['w16'][row(h)], s16[h])))
        o = _dot(p['qk16'], vn16) + stack(lambda h: _dot(p['qd16'][row(h)], s16[h]))
        states = [states[h] * p['cdec'][h] + _dot_tn(p['kd16'][row(h)], vn16[row(h)])
                  for h in range(N_HEADS)]
        for h in range(N_HEADS):
            oh = o[row(h)]
            ms = jnp.mean(oh * oh, axis=-1, keepdims=True)
            y = oh * lax.rsqrt(ms + NORM_EPS) * nrm_ref[...]
            o_ref[rs, hsl(h)] = (y * _silu(gate_ref[rs, hsl(h)])).astype(o_ref.dtype)
    for h in range(N_HEADS):
        state_ref[h] = states[h]


def _gated_deltanet(gq, p32, prm, gnorm, batch, seq):
    lb = min(256, seq)
    nb = seq // lb
    t = batch * seq
    blk = lambda cb: pl.BlockSpec((lb, BRANCH_W), lambda b, j, cb=cb: (b * nb + j, cb))
    return pl.pallas_call(
        functools.partial(_gdn_kernel, n_chunks=lb // GDN_CHUNK),
        grid=(batch, nb),
        in_specs=[blk(0), blk(1), blk(2), blk(7),
                  pl.BlockSpec((lb, LANES), lambda b, j: (b * nb + j, P32_SMALL_BLK)),
                  pl.BlockSpec((SUBLANES, LANES), lambda b, j: (0, 0)),
                  pl.BlockSpec((1, HEAD_DIM), lambda b, j: (0, 0))],
        out_specs=pl.BlockSpec((lb, BRANCH_W), lambda b, j: (b * nb + j, 0)),
        out_shape=jax.ShapeDtypeStruct((t, BRANCH_W), BF16),
        scratch_shapes=[pltpu.VMEM((N_HEADS, HEAD_DIM, HEAD_DIM), F32)],
        compiler_params=_cparams(("parallel", "arbitrary")),
        name="gated_deltanet",
    )(gq, gq, gq, p32, p32, prm, gnorm)


def _merge_kernel(x_ref, ya_ref, yb_ref, yc_ref, yd_ref, wg_ref, wb_ref, o_ref):
    x = x_ref[...]
    acc = None
    for i, y_ref in enumerate((ya_ref, yb_ref, yc_ref, yd_ref)):
        term = _sigmoid(_dot(x, wg_ref[i])) * _dot(y_ref[...], wb_ref[i])
        acc = term if acc is None else acc + term
    o_ref[...] = acc.astype(o_ref.dtype)


def _merge(x16, ys, wgl, wbl):
    (wg16, l), (wb16, _) = wgl, wbl
    t = x16.shape[0]
    tm = min(1024, t)
    tn = 256
    yspec = pl.BlockSpec((tm, BRANCH_W), lambda i, j: (i, 0))
    return pl.pallas_call(
        _merge_kernel,
        grid=(t // tm, D_MODEL // tn),
        in_specs=[pl.BlockSpec((tm, D_MODEL), lambda i, j: (i, 0)), yspec, yspec, yspec, yspec,
                  pl.BlockSpec((None, 4, D_MODEL, tn), lambda i, j: (l, 0, 0, j)),
                  pl.BlockSpec((None, 4, BRANCH_W, tn), lambda i, j: (l, 0, 0, j))],
        out_specs=pl.BlockSpec((tm, tn), lambda i, j: (i, j)),
        out_shape=jax.ShapeDtypeStruct((t, D_MODEL), BF16),
        compiler_params=_cparams(("parallel", "arbitrary")),
        name="gated_merge",
    )(x16, *ys, wg16, wb16)


def _layer_norm_rows(h, g, b):
    mu = jnp.mean(h, axis=-1, keepdims=True)
    d = h - mu
    var = jnp.mean(d * d, axis=-1, keepdims=True)
    return d * lax.rsqrt(var + NORM_EPS) * g + b


def _outproj_ln_kernel(m_ref, w_ref, x_ref, g_ref, b_ref, o32_ref, o16_ref):
    h = ALPHA * x_ref[...] + _dot(m_ref[...], w_ref[...])
    y = _layer_norm_rows(h, g_ref[...], b_ref[...])
    o32_ref[...] = y
    o16_ref[...] = _bf(y)


def _outproj_ln(merged16, woutl, x32, g, b):
    w_out16, l = woutl
    t = x32.shape[0]
    tm = min(256, t)
    row = pl.BlockSpec((tm, D_MODEL), lambda i: (i, 0))
    vec = pl.BlockSpec((1, D_MODEL), lambda i: (0, 0))
    return pl.pallas_call(
        _outproj_ln_kernel,
        grid=(t // tm,),
        in_specs=[row, pl.BlockSpec((None, D_MODEL, D_MODEL), lambda i: (l, 0, 0)), row, vec, vec],
        out_specs=[row, row],
        out_shape=[jax.ShapeDtypeStruct((t, D_MODEL), F32), jax.ShapeDtypeStruct((t, D_MODEL), BF16)],
        compiler_params=_cparams(("parallel",)),
        name="outproj_ln",
    )(merged16, w_out16, x32, g, b)


def _router_kernel(x_ref, whi_ref, wlo_ref, b_ref, ids_ref, gates_ref, sizes_ref, run_ref):
    @pl.when(pl.program_id(0) == 0)
    def _():
        run_ref[...] = jnp.zeros_like(run_ref)

    xh, xl = _split2(x_ref[...])
    whi = whi_ref[...]
    logits = _dot(xh, whi) + _dot(xh, wlo_ref[...]) + _dot(xl, whi) + b_ref[...]
    tm = logits.shape[0]
    lane = lax.broadcasted_iota(I32, (tm, LANES), 1)
    ninf = -jnp.inf
    gl = jnp.where(lane < N_GROUPS, logits, ninf)
    gmax = jnp.max(gl, axis=-1, keepdims=True)
    grp = jnp.min(jnp.where(gl == gmax, lane, LANES), axis=-1, keepdims=True)
    p_grp = 1.0 / jnp.sum(jnp.where(lane < N_GROUPS, jnp.exp(logits - gmax), 0.0), axis=-1, keepdims=True)
    lo = N_GROUPS + grp * EXPERTS_PER_GROUP
    el = jnp.where((lane >= lo) & (lane < lo + EXPERTS_PER_GROUP), logits, ninf)
    v1 = jnp.max(el, axis=-1, keepdims=True)
    i1 = jnp.min(jnp.where(el == v1, lane, LANES), axis=-1, keepdims=True)
    el2 = jnp.where(lane == i1, ninf, el)
    v2 = jnp.max(el2, axis=-1, keepdims=True)
    i2 = jnp.min(jnp.where(el2 == v2, lane, LANES), axis=-1, keepdims=True)
    e21 = jnp.exp(v2 - v1)
    den = 1.0 / (1.0 + e21)
    g1 = p_grp * den
    g2 = p_grp * (e21 * den)
    oh = jnp.where(lane == i1, 1.0, jnp.where(lane == i2, 1.0, 0.0))
    tr = lax.broadcasted_iota(I32, (tm, tm), 0)
    tc = lax.broadcasted_iota(I32, (tm, tm), 1)
    earlier = jnp.where(tr > tc, 1.0, 0.0).astype(BF16)
    before = _dot(earlier, _bf(oh)) + run_ref[0:1, :]
    r1 = jnp.sum(jnp.where(lane == i1, before, 0.0), axis=-1, keepdims=True).astype(I32)
    r2 = jnp.sum(jnp.where(lane == i2, before, 0.0), axis=-1, keepdims=True).astype(I32)
    run_new = before[tm - 1:tm, :] + oh[tm - 1:tm, :]
    run_ref[0:1, :] = run_new
    sizes_ref[...] = jnp.broadcast_to(run_new, sizes_ref.shape)
    ids_ref[...] = jnp.where(lane == 0, i1 - N_GROUPS, jnp.where(lane == 1, i2 - N_GROUPS,
                             jnp.where(lane == 2, r1, jnp.where(lane == 3, r2, 0))))
    gates_ref[...] = jnp.where(lane == 0, g1, jnp.where(lane == 1, g2, 0.0))


def _router(x32, whi, wlo, bias):
    t = x32.shape[0]
    tm = min(512, t)
    row = pl.BlockSpec((tm, LANES), lambda i: (i, 0))
    wsp = pl.BlockSpec((D_MODEL, LANES), lambda i: (0, 0))
    return pl.pallas_call(
        _router_kernel,
        grid=(t // tm,),
        in_specs=[pl.BlockSpec((tm, D_MODEL), lambda i: (i, 0)), wsp, wsp,
                  pl.BlockSpec((1, LANES), lambda i: (0, 0))],
        out_specs=[row, row, pl.BlockSpec((SUBLANES, LANES), lambda i: (0, 0))],
        out_shape=[jax.ShapeDtypeStruct((t, LANES), I32), jax.ShapeDtypeStruct((t, LANES), F32),
                   jax.ShapeDtypeStruct((SUBLANES, LANES), F32)],
        scratch_shapes=[pltpu.VMEM((SUBLANES, LANES), F32)],
        compiler_params=_cparams(("arbitrary",)),
        name="moe_router",
    )(x32, whi, wlo, bias)


DMA_UNROLL = 8


def _dispatch_kernel(slot_ref, x_hbm, xb_init, xb_hbm, sem, *, tm):
    del xb_init
    i = pl.program_id(0)
    n = pl.num_programs(0)
    par = lax.rem(i, 2)

    def row_copy(tok, slot, p):
        return pltpu.make_async_copy(x_hbm.at[pl.ds(tok, 1), :], xb_hbm.at[pl.ds(slot, 1), :], sem.at[p])

    def issue(g, _):
        for u in range(DMA_UNROLL):
            r = g * DMA_UNROLL + u
            row_copy(i * tm + r, slot_ref[0, 0, 2 * r], par).start()
            row_copy(i * tm + r, slot_ref[0, 0, 2 * r + 1], par).start()
        return 0

    lax.fori_loop(0, tm // DMA_UNROLL, issue, 0)

    def drain(p):
        def wait(g, _):
            for u in range(2 * DMA_UNROLL):
                row_copy(0, 0, p).wait()
            return 0
        lax.fori_loop(0, tm // DMA_UNROLL, wait, 0)

    @pl.when(i > 0)
    def _():
        drain(1 - par)

    @pl.when(i == n - 1)
    def _():
        drain(par)


def _dispatch(x32, slots, n_slots):
    t = x32.shape[0]
    tm = min(512, t)
    anyspec = pl.BlockSpec(memory_space=pl.ANY)
    return pl.pallas_call(
        functools.partial(_dispatch_kernel, tm=tm),
        grid=(t // tm,),
        in_specs=[pl.BlockSpec((1, 1, 2 * tm), lambda i: (i, 0, 0), memory_space=pltpu.SMEM),
                  anyspec, anyspec],
        out_specs=anyspec,
        out_shape=jax.ShapeDtypeStruct((n_slots, D_MODEL), F32),
        scratch_shapes=[pltpu.SemaphoreType.DMA((2,))],
        input_output_aliases={2: 0},
        compiler_params=_cparams(("arbitrary",)),
        name="moe_dispatch",
    )(slots.reshape(t // tm, 1, 2 * tm), x32, jnp.zeros((n_slots, D_MODEL), F32))


def _expert_kernel(be_ref, nb_ref, x_ref, wg_ref, wu_ref, wd_ref, o_ref):
    i = pl.program_id(0)

    @pl.when(i < nb_ref[0])
    def _():
        xe = _bf(x_ref[...])
        h = _silu(_dot(xe, wg_ref[...])) * _dot(xe, wu_ref[...])
        o_ref[...] = _dot(_bf(h), wd_ref[...])

    @pl.when(i >= nb_ref[0])
    def _():
        o_ref[...] = jnp.zeros_like(o_ref)


def _experts(xb, block_e, nb_used, wgl, wul, wdl, blk):
    (wg16, l), (wu16, _), (wd16, _) = wgl, wul, wdl
    n_blocks = block_e.shape[0]
    wspec = lambda shape: pl.BlockSpec((None, None) + shape, lambda i, be, nb: (l, be[i], 0, 0))
    grid_spec = pltpu.PrefetchScalarGridSpec(
        num_scalar_prefetch=2,
        grid=(n_blocks,),
        in_specs=[pl.BlockSpec((blk, D_MODEL), lambda i, be, nb: (i, 0)),
                  wspec((D_MODEL, D_EXPERT)), wspec((D_MODEL, D_EXPERT)), wspec((D_EXPERT, D_MODEL))],
        out_specs=pl.BlockSpec((blk, D_MODEL), lambda i, be, nb: (i, 0)),
    )
    return pl.pallas_call(
        _expert_kernel,
        grid_spec=grid_spec,
        out_shape=jax.ShapeDtypeStruct((n_blocks * blk, D_MODEL), F32),
        compiler_params=_cparams(("arbitrary",)),
        name="moe_experts",
    )(block_e, nb_used, xb, wg16, wu16, wd16)


def _combine_kernel(slot_ref, slot_next_ref, y_hbm, x_ref, gates_ref, g_ref, b_ref, o32_ref, o16_ref,
                    ybuf, sem, *, tm):
    i = pl.program_id(0)
    n = pl.num_programs(0)
    par = lax.rem(i, 2)

    def row_copy(r, k, slot, p):
        return pltpu.make_async_copy(y_hbm.at[pl.ds(slot, 1), :], ybuf.at[p, k, pl.ds(r, 1), :], sem.at[p])

    def issue(sref, p):
        def body(g, _):
            for u in range(DMA_UNROLL):
                r = g * DMA_UNROLL + u
                row_copy(r, 0, sref[0, 0, 2 * r], p).start()
                row_copy(r, 1, sref[0, 0, 2 * r + 1], p).start()
            return 0
        lax.fori_loop(0, tm // DMA_UNROLL, body, 0)

    @pl.when(i == 0)
    def _():
        issue(slot_ref, 0)

    @pl.when(i + 1 < n)
    def _():
        issue(slot_next_ref, 1 - par)

    def wait(g, _):
        for u in range(2 * DMA_UNROLL):
            row_copy(0, 0, 0, par).wait()
        return 0

    lax.fori_loop(0, tm // DMA_UNROLL, wait, 0)
    gates = gates_ref[...]
    y = ybuf[par, 0] * gates[:, 0:1] + ybuf[par, 1] * gates[:, 1:2]
    out = _layer_norm_rows(ALPHA * x_ref[...] + y, g_ref[...], b_ref[...])
    o32_ref[...] = out
    o16_ref[...] = _bf(out)


def _combine_ln(slots, yb, x32, gates, g, b):
    t = x32.shape[0]
    tm = min(256, t)
    n = t // tm
    row = pl.BlockSpec((tm, D_MODEL), lambda i: (i, 0))
    vec = pl.BlockSpec((1, D_MODEL), lambda i: (0, 0))
    slots3 = slots.reshape(n, 1, 2 * tm)
    return pl.pallas_call(
        functools.partial(_combine_kernel, tm=tm),
        grid=(n,),
        in_specs=[pl.BlockSpec((1, 1, 2 * tm), lambda i: (i, 0, 0), memory_space=pltpu.SMEM),
                  pl.BlockSpec((1, 1, 2 * tm), lambda i: (jnp.minimum(i + 1, n - 1), 0, 0),
                               memory_space=pltpu.SMEM),
                  pl.BlockSpec(memory_space=pl.ANY), row,
                  pl.BlockSpec((tm, LANES), lambda i: (i, 0)), vec, vec],
        out_specs=[row, row],
        out_shape=[jax.ShapeDtypeStruct((t, D_MODEL), F32), jax.ShapeDtypeStruct((t, D_MODEL), BF16)],
        scratch_shapes=[pltpu.VMEM((2, 2, tm, D_MODEL), F32), pltpu.SemaphoreType.DMA((2,))],
        compiler_params=_cparams(("arbitrary",)),
        name="moe_combine_ln",
    )(slots3, slots3, yb, x32, gates, g, b)


def _moe_plan(ids, sizes_row, blk):
    t = ids.shape[0]
    sizes = sizes_row[0, N_GROUPS:N_GROUPS + N_EXPERTS].astype(I32)
    padded = (sizes + blk - 1) // blk * blk
    pad_ends = jnp.cumsum(padded)
    pad_starts = pad_ends - padded
    slots = (pad_starts[ids[:, 0:2]] + ids[:, 2:4]).astype(I32)
    n_blocks = 2 * t // blk + N_EXPERTS
    block_e = jnp.minimum(jnp.searchsorted(pad_ends, jnp.arange(n_blocks, dtype=I32) * blk, side='right'),
                          N_EXPERTS - 1).astype(I32)
    nb_used = (pad_ends[-1:] // blk).astype(I32)
    return slots, block_e, nb_used, n_blocks * blk


MOE_BLK = 256
STACKED = ('w16', 'w32', 'wg', 'wb', 'wout', 'eg', 'eu', 'ed')
IN_TM, IN_TN = 1024, 512


def _prep_weights(w_in, dsa_q_norm, w_uq, gdn_conv, gdn_a_log, gdn_dt_bias, gdn_norm, w_branch,
                  w_branch_gate, w_out, ln1_g, ln1_b, w_router_group, b_router_group,
                  w_router_expert, b_router_expert, w_exp_gate, w_exp_up, w_exp_down, ln2_g, ln2_b):
    depth = w_in.shape[0]
    zc = lambda n: jnp.zeros((depth, D_MODEL, n), F32)
    c = lambda a, b: w_in[..., a:b]
    w16 = jnp.concatenate([c(0, 1536), c(3968, 4224), c(4224, 4288), zc(P16_W - 1856)], -1).astype(BF16)
    w32 = jnp.concatenate([c(4296, 5832), c(1536, 3584), c(5840, 6352), c(3584, 3968),
                           c(4288, 4296), c(5832, 5836), c(5836, 5840), zc(LANES - 16)], -1).astype(BF16)
    qi = w_uq[..., BRANCH_W:].reshape(depth, DSA_Q_RANK, IDX_HEADS, IDX_DIM)
    qi = jnp.pad(qi, ((0, 0), (0, 0), (0, 0), (0, LANES - IDX_DIM))).reshape(depth, DSA_Q_RANK, IDX_HEADS * LANES)
    wuq = jnp.concatenate([qi, w_uq[..., :BRANCH_W]], -1).astype(BF16)
    wr = jnp.concatenate([w_router_group, w_router_expert,
                          zc(LANES - N_GROUPS - N_EXPERTS)], -1)
    rhi = wr.astype(BF16)
    rlo = (wr - rhi.astype(F32)).astype(BF16)
    rb = jnp.concatenate([b_router_group, b_router_expert,
                          jnp.zeros((depth, LANES - N_GROUPS - N_EXPERTS), F32)], -1)[:, None, :]
    prm = jnp.zeros((depth, SUBLANES, LANES), F32)
    prm = prm.at[:, 0, SMALL_A:SMALL_A + N_HEADS].set(gdn_a_log)
    prm = prm.at[:, 1, SMALL_A:SMALL_A + N_HEADS].set(gdn_dt_bias)
    return dict(
        w16=w16, w32=w32, q_norm=dsa_q_norm[:, None, :], wuq=wuq, conv=gdn_conv, prm=prm,
        gnorm=gdn_norm[:, None, :], wb=w_branch.astype(BF16), wg=w_branch_gate.astype(BF16),
        wout=w_out.astype(BF16), ln1g=ln1_g[:, None, :], ln1b=ln1_b[:, None, :], rhi=rhi, rlo=rlo, rb=rb,
        eg=w_exp_gate.astype(BF16), eu=w_exp_up.astype(BF16), ed=w_exp_down.astype(BF16),
        ln2g=ln2_g[:, None, :], ln2b=ln2_b[:, None, :])


def _mixer(x16, lw, batch, seq):
    p16 = _matmul(x16, lw['w16'], BF16, IN_TM, IN_TN, "in_proj_bf16")
    p32 = _matmul(x16, lw['w32'], F32, IN_TM, IN_TN, "in_proj_f32")
    y_a = _sb_attention(p16, batch, seq)
    y_b = _retention(p32, batch, seq)
    q_all = _dsa_query(p32, lw['q_norm'], lw['wuq'])
    y_c = _dsa_attention(q_all, p32, p16, batch, seq)
    gq = _gdn_conv(p32, lw['conv'], batch, seq)
    y_d = _gated_deltanet(gq, p32, lw['prm'], lw['gnorm'], batch, seq)
    return _merge(x16, (y_a, y_b, y_c, y_d), lw['wg'], lw['wb'])


def _moe(x32, lw):
    ids, gates, sizes = _router(x32, lw['rhi'], lw['rlo'], lw['rb'])
    slots, block_e, nb_used, n_slots = _moe_plan(ids, sizes, MOE_BLK)
    xb = _dispatch(x32, slots, n_slots)
    yb = _experts(xb, block_e, nb_used, lw['eg'], lw['eu'], lw['ed'], MOE_BLK)
    return _combine_ln(slots, yb, x32, gates, lw['ln2g'], lw['ln2b'])


def _layer_weights(ws, l):
    return {k: ((v, l) if k in STACKED else v[l]) for k, v in ws.items()}


def _layer(x32, x16, lw, batch, seq):
    merged = _mixer(x16, lw, batch, seq)
    x32, x16 = _outproj_ln(merged, lw['wout'], x32, lw['ln1g'], lw['ln1b'])
    return _moe(x32, lw)


def kernel(x, w_in, dsa_q_norm, w_uq, gdn_conv, gdn_a_log, gdn_dt_bias, gdn_norm, w_branch, w_branch_gate,
           w_out, ln1_g, ln1_b, w_router_group, b_router_group, w_router_expert, b_router_expert,
           w_exp_gate, w_exp_up, w_exp_down, ln2_g, ln2_b):
    batch, seq, d = x.shape
    assert d == D_MODEL and seq % RET_CHUNK == 0
    ws = _prep_weights(w_in, dsa_q_norm, w_uq, gdn_conv, gdn_a_log, gdn_dt_bias, gdn_norm, w_branch,
                       w_branch_gate, w_out, ln1_g, ln1_b, w_router_group, b_router_group,
                       w_router_expert, b_router_expert, w_exp_gate, w_exp_up, w_exp_down, ln2_g, ln2_b)
    x32 = x.reshape(batch * seq, d)
    x16 = x32.astype(BF16)
    for l in range(w_in.shape[0]):
        x32, x16 = _layer(x32, x16, _layer_weights(ws, l), batch, seq)
    return x32.reshape(batch, seq, d)
```

```python
import functools
import math

import numpy as np
import jax
import jax.numpy as jnp
from jax import lax
from jax.experimental import pallas as pl
from jax.experimental.pallas import tpu as pltpu

F32 = jnp.float32
BF16 = jnp.bfloat16
I32 = jnp.int32

D_MODEL = 2048
DEPTH = 4
HEAD_DIM = 128
N_HEADS = 4
BRANCH_W = N_HEADS * HEAD_DIM
RET_CHUNK = 128
DSA_Q_RANK = 384
DSA_TOPK = 256
IDX_HEADS = 8
IDX_DIM = 64
GDN_CONV = 4
GDN_CHUNK = 64
N_GROUPS = 4
EXPERTS_PER_GROUP = 8
N_EXPERTS = N_GROUPS * EXPERTS_PER_GROUP
D_EXPERT = 512
ALPHA = (2.0 * DEPTH) ** 0.25
NORM_EPS = 1e-5

LANES = 128
SUBLANES = 8
VMEM_LIMIT_BYTES = 56 * 1024 * 1024

P16_W = 2048
P16_DSA_K_BLK = 12
P16_DSA_V_BLK = 13
P16_IDXK_BLK = 14
P32_W = 4608
P32_SMALL_BLK = 35
SMALL_IDXW, SMALL_B, SMALL_A = 0, 8, 12
UQ_W = IDX_HEADS * LANES + BRANCH_W
UQ_Q_BLK = IDX_HEADS * LANES // BRANCH_W

INT_MIN = -2147483648


def _cparams(sem):
    return pltpu.CompilerParams(dimension_semantics=sem, vmem_limit_bytes=VMEM_LIMIT_BYTES)


def _bf(x):
    return x.astype(BF16)


def _dot(a, b):
    return jnp.dot(a, b, preferred_element_type=F32)


def _dot_nt(a, b):
    return lax.dot_general(a, b, (((1,), (1,)), ((), ())), preferred_element_type=F32)


def _dot_tn(a, b):
    return lax.dot_general(a, b, (((0,), (0,)), ((), ())), preferred_element_type=F32)


def _split2(x):
    hi = _bf(x)
    lo = _bf(x - hi.astype(F32))
    return hi, lo


def _split3(x):
    hi = _bf(x)
    r = x - hi.astype(F32)
    mid = _bf(r)
    lo = _bf(r - mid.astype(F32))
    return hi, mid, lo


def _dot_left_exact(m01, x):
    hi, mid, lo = _split3(x)
    return _dot(m01, hi) + _dot(m01, mid) + _dot(m01, lo)


def _dot_right_exact(x, m01):
    hi, mid, lo = _split3(x)
    return _dot(hi, m01) + _dot(mid, m01) + _dot(lo, m01)


def _dot_hp(a, b):
    ah, al = _split2(a)
    bh, bl = _split2(b)
    return _dot(ah, bh) + _dot(ah, bl) + _dot(al, bh)


def _silu(x):
    return x * (1.0 / (1.0 + jnp.exp(-x)))


def _sigmoid(x):
    return 1.0 / (1.0 + jnp.exp(-x))


def _softplus(x):
    return jnp.maximum(x, 0.0) + jnp.log1p(jnp.exp(-jnp.abs(x)))


def _mm_kernel(x_ref, w_ref, o_ref):
    o_ref[...] = _dot(x_ref[...], w_ref[...]).astype(o_ref.dtype)


def _matmul(x, wl, out_dtype, tm, tn, name):
    w, l = wl
    m, k = x.shape
    n = w.shape[2]
    tm = min(tm, m)
    return pl.pallas_call(
        _mm_kernel,
        grid=(m // tm, n // tn),
        in_specs=[pl.BlockSpec((tm, k), lambda i, j: (i, 0)),
                  pl.BlockSpec((None, k, tn), lambda i, j: (l, 0, j))],
        out_specs=pl.BlockSpec((tm, tn), lambda i, j: (i, j)),
        out_shape=jax.ShapeDtypeStruct((m, n), out_dtype),
        compiler_params=_cparams(("parallel", "arbitrary")),
        name=name,
    )(x, w)


def _sb_kernel(q_ref, k_ref, v_ref, o_ref, *, tq):
    i = pl.program_id(1)
    scale = HEAD_DIM ** -0.5
    hsl = lambda h: slice(h * HEAD_DIM, (h + 1) * HEAD_DIM)
    qs = [q_ref[:, hsl(h)] for h in range(N_HEADS)]
    row = i * tq + lax.broadcasted_iota(I32, (tq, 1), 0)
    col0 = lax.broadcasted_iota(I32, (1, tq), 1)
    jr = lax.broadcasted_iota(I32, (tq, tq), 0)
    jc = lax.broadcasted_iota(I32, (tq, tq), 1)
    later = jnp.where(jr > jc, 1.0, 0.0).astype(BF16)

    def body(jj, carry):
        kb = i - jj
        ks = pl.multiple_of(kb * tq, tq)
        causal = (kb * tq + col0) < row
        out = []
        for h in range(N_HEADS):
            acc, run = carry[h]
            kblk = k_ref[pl.ds(ks, tq), hsl(h)]
            vblk = v_ref[pl.ds(ks, tq), hsl(h)]
            z = _dot_nt(qs[h], kblk) * scale
            sp = jnp.log(1.0 + jnp.exp(-jnp.abs(z)))
            log_beta = jnp.minimum(z, 0.0) - sp
            log_1m = jnp.where(causal, log_beta - z, 0.0)
            hi, lo = _split2(log_1m)
            suffix = _dot(hi, later) + _dot(lo, later)
            w = jnp.where(causal, jnp.exp(log_beta + suffix + run), 0.0)
            out.append((acc + _dot(_bf(w), vblk), run + suffix[:, 0:1] + log_1m[:, 0:1]))
        return tuple(out)

    init = tuple((jnp.zeros((tq, HEAD_DIM), F32), jnp.zeros((tq, 1), F32)) for _ in range(N_HEADS))
    res = lax.fori_loop(0, i + 1, body, init)
    for h in range(N_HEADS):
        o_ref[:, hsl(h)] = res[h][0].astype(o_ref.dtype)


def _sb_attention(p16, batch, seq):
    tq = min(256, seq)
    nq = seq // tq
    t = batch * seq
    return pl.pallas_call(
        functools.partial(_sb_kernel, tq=tq),
        grid=(batch, nq),
        in_specs=[pl.BlockSpec((tq, BRANCH_W), lambda b, i: (b * nq + i, 0)),
                  pl.BlockSpec((seq, BRANCH_W), lambda b, i: (b, 1)),
                  pl.BlockSpec((seq, BRANCH_W), lambda b, i: (b, 2))],
        out_specs=pl.BlockSpec((tq, BRANCH_W), lambda b, i: (b * nq + i, 0)),
        out_shape=jax.ShapeDtypeStruct((t, BRANCH_W), BF16),
        compiler_params=_cparams(("parallel", "arbitrary")),
        name="sb_attention",
    )(p16, p16, p16)


def _ret_kernel(dch_ref, q_ref, k_ref, v_ref, g_ref, cos_ref, sin_ref, dintra_ref, dq_ref, dk_ref,
                o_ref, state_ref, *, n_chunks):
    c = RET_CHUNK

    @pl.when(pl.program_id(1) == 0)
    def _():
        state_ref[...] = jnp.zeros_like(state_ref)

    def rot(x, cs, sn):
        return x * cs + pltpu.roll(x, HEAD_DIM // 2, 1) * sn

    for h in range(N_HEADS):
        hs = slice(h * HEAD_DIM, (h + 1) * HEAD_DIM)
        state = state_ref[h]
        for ci in range(n_chunks):
            rs = slice(ci * c, (ci + 1) * c)
            cs = cos_ref[rs, :]
            sn = sin_ref[rs, :]
            qc = rot(q_ref[rs, hs], cs, sn)
            kc = rot(k_ref[rs, hs], cs, sn) * (HEAD_DIM ** -0.5)
            vc = _bf(v_ref[rs, hs])
            qb = _bf(qc)
            s = _dot_nt(qb, _bf(kc)) * dintra_ref[h]
            o = _dot(_bf(s), vc) + _dot(qb, _bf(state)) * dq_ref[h]
            kv = _dot_tn(_bf(kc * dk_ref[h]), vc)
            state = dch_ref[h] * state + kv
            mu = jnp.mean(o, axis=-1, keepdims=True)
            d = o - mu
            var = jnp.mean(d * d, axis=-1, keepdims=True)
            y = d * lax.rsqrt(var + NORM_EPS)
            o_ref[rs, hs] = (y * _silu(g_ref[rs, hs])).astype(o_ref.dtype)
        state_ref[h] = state


def _retention(p32, batch, seq):
    lb = min(512, seq)
    nb = seq // lb
    t = batch * seq
    f32 = F32
    hh = N_HEADS
    c = RET_CHUNK
    log_gamma = jnp.log1p(-jnp.exp2(-5.0 - jnp.arange(hh, dtype=f32)))
    pos = jnp.arange(c, dtype=f32)
    rel = pos[:, None] - pos[None, :]
    causal = rel >= 0
    d_intra = jnp.where(causal, jnp.exp(jnp.where(causal, rel, 0.0) * log_gamma[:, None, None]), 0.0)
    d_q = jnp.exp((pos + 1.0) * log_gamma[:, None])
    d_k = jnp.exp((c - 1.0 - pos) * log_gamma[:, None])
    d_chunk = jnp.exp(c * log_gamma)
    dq_b = jnp.broadcast_to(d_q[:, :, None], (hh, c, HEAD_DIM))
    dk_b = jnp.broadcast_to(d_k[:, :, None], (hh, c, HEAD_DIM))
    inv = 1.0 / (10000.0 ** (jnp.arange(0, HEAD_DIM, 2, dtype=f32) / HEAD_DIM))
    ang = jnp.arange(seq).astype(f32)[:, None] * inv[None, :]
    cos_t = jnp.concatenate([jnp.cos(ang), jnp.cos(ang)], -1)
    sin_t = jnp.concatenate([-jnp.sin(ang), jnp.sin(ang)], -1)

    qkvg = lambda blk: pl.BlockSpec((lb, BRANCH_W), lambda b, j, blk=blk: (b * nb + j, blk))
    tab = pl.BlockSpec((lb, HEAD_DIM), lambda b, j: (j, 0))
    whole3 = pl.BlockSpec((hh, c, HEAD_DIM), lambda b, j: (0, 0, 0))
    return pl.pallas_call(
        functools.partial(_ret_kernel, n_chunks=lb // c),
        grid=(batch, nb),
        in_specs=[pl.BlockSpec(memory_space=pltpu.SMEM),
                  qkvg(3), qkvg(4), qkvg(5), qkvg(6), tab, tab, whole3, whole3, whole3],
        out_specs=pl.BlockSpec((lb, BRANCH_W), lambda b, j: (b * nb + j, 0)),
        out_shape=jax.ShapeDtypeStruct((t, BRANCH_W), BF16),
        scratch_shapes=[pltpu.VMEM((hh, HEAD_DIM, HEAD_DIM), F32)],
        compiler_params=_cparams(("parallel", "arbitrary")),
        name="retention",
    )(d_chunk, p32, p32, p32, p32, cos_t, sin_t, d_intra, dq_b, dk_b)


def _uq_kernel(x_ref, g_ref, w_ref, o_ref):
    x = x_ref[:, :DSA_Q_RANK]
    ms = jnp.mean(x * x, axis=-1, keepdims=True)
    xn = x * lax.rsqrt(ms + NORM_EPS) * g_ref[...]
    o_ref[...] = _dot(_bf(xn), w_ref[...]).astype(o_ref.dtype)


def _dsa_query(p32, q_norm, w_uq16):
    t = p32.shape[0]
    tm = min(512, t)
    return pl.pallas_call(
        _uq_kernel,
        grid=(t // tm,),
        in_specs=[pl.BlockSpec((tm, 512), lambda i: (i, 8)),
                  pl.BlockSpec((1, DSA_Q_RANK), lambda i: (0, 0)),
                  pl.BlockSpec((DSA_Q_RANK, UQ_W), lambda i: (0, 0))],
        out_specs=pl.BlockSpec((tm, UQ_W), lambda i: (i, 0)),
        out_shape=jax.ShapeDtypeStruct((t, UQ_W), BF16),
        compiler_params=_cparams(("parallel",)),
        name="dsa_query",
    )(p32, q_norm, w_uq16)


def _dsa_kernel(q_ref, qi_ref, small_ref, kidx_ref, k_ref, v_ref, o_ref, keys_ref,
                *, tq, kc, topk):
    i = pl.program_id(1)
    nkv = ((i + 1) * tq + kc - 1) // kc
    qpos = i * tq + lax.broadcasted_iota(I32, (1, tq), 1)
    kofs = lax.broadcasted_iota(I32, (kc, 1), 0)
    w_t = jnp.transpose(small_ref[...] * (IDX_HEADS ** -0.5) * (IDX_DIM ** -0.5))
    wrows = [w_t[SMALL_IDXW + h:SMALL_IDXW + h + 1, :] for h in range(IDX_HEADS)]
    qis = [qi_ref[:, h * LANES:(h + 1) * LANES] for h in range(IDX_HEADS)]

    def score_chunk(j, _):
        ks = pl.multiple_of(j * kc, kc)
        kix = kidx_ref[pl.ds(ks, kc), :]
        acc = jnp.zeros((kc, tq), F32)
        for h in range(IDX_HEADS):
            acc = acc + jnp.maximum(_dot_nt(kix, qis[h]), 0.0) * wrows[h]
        bits = pltpu.bitcast(acc, I32)
        key = jnp.where(bits < 0, bits ^ 0x7FFFFFFF, bits)
        causal = (j * kc + kofs) <= qpos
        keys_ref[j] = jnp.where(causal, key, INT_MIN)
        return 0

    lax.fori_loop(0, nkv, score_chunk, 0)

    def count_ge(cand):
        def cbody(j, part):
            hit = jnp.where(keys_ref[j] >= cand, 1.0, 0.0).reshape(kc // SUBLANES, SUBLANES, tq)
            return part + jnp.sum(hit, axis=0)
        part = lax.fori_loop(0, nkv, cbody, jnp.zeros((SUBLANES, tq), F32))
        return jnp.sum(part, axis=0, keepdims=True)

    def bit_step(it, theta):
        cand = theta + lax.shift_left(jnp.int32(1), 31 - it)
        return jnp.where(count_ge(cand) >= topk, cand, theta)

    theta = lax.fori_loop(0, 32, bit_step, jnp.full((1, tq), INT_MIN, I32))
    n_gt = jnp.where(theta == 2147483647, 0.0, count_ge(theta + 1))
    need = jnp.where(theta == INT_MIN, 0.0, topk - n_gt)

    ur = lax.broadcasted_iota(I32, (kc, kc), 0)
    uc = lax.broadcasted_iota(I32, (kc, kc), 1)
    upto = jnp.where(ur >= uc, 1.0, 0.0).astype(BF16)
    qs = [q_ref[:, h * HEAD_DIM:(h + 1) * HEAD_DIM] for h in range(N_HEADS)]
    scale = HEAD_DIM ** -0.5
    ninf = -jnp.inf

    def attend(j, carry):
        seen, ms, ls, accs = carry
        ks = pl.multiple_of(j * kc, kc)
        key = keys_ref[j]
        eq = key == theta
        pc = _dot(upto, jnp.where(eq, 1.0, 0.0).astype(BF16)) + seen
        rank = jnp.where(eq, pc - need, jnp.where(key > theta, -1.0, 1.0))
        sel = rank <= 0.0
        seen = pc[kc - 1:kc, :]
        kblk = k_ref[pl.ds(ks, kc), :]
        v_t = _bf(jnp.transpose(v_ref[pl.ds(ks, kc), :].astype(F32)))
        ms2, ls2, accs2 = [], [], []
        for h in range(N_HEADS):
            logit = jnp.where(sel, _dot_nt(kblk, qs[h]) * scale, ninf)
            m_new = jnp.maximum(ms[h], jnp.max(logit, axis=0, keepdims=True))
            p = jnp.exp(logit - m_new)
            a = jnp.exp(ms[h] - m_new)
            ls2.append(a * ls[h] + jnp.sum(p, axis=0, keepdims=True))
            accs2.append(a * accs[h] + _dot(v_t, _bf(p)))
            ms2.append(m_new)
        return seen, tuple(ms2), tuple(ls2), tuple(accs2)

    init = (jnp.zeros((1, tq), F32),
            tuple(jnp.full((1, tq), -1e30, F32) for _ in range(N_HEADS)),
            tuple(jnp.zeros((1, tq), F32) for _ in range(N_HEADS)),
            tuple(jnp.zeros((HEAD_DIM, tq), F32) for _ in range(N_HEADS)))
    _, _, ls, accs = lax.fori_loop(0, nkv, attend, init)
    for h in range(N_HEADS):
        o_ref[:, h * HEAD_DIM:(h + 1) * HEAD_DIM] = jnp.transpose(accs[h] / ls[h]).astype(o_ref.dtype)


def _dsa_attention(q_all, p32, p16, batch, seq):
    tq = min(256, seq)
    kc = min(512, seq)
    nq = seq // tq
    t = batch * seq
    topk = min(DSA_TOPK, seq // 4)
    kv = lambda blk: pl.BlockSpec((seq, LANES), lambda b, i, blk=blk: (b, blk))
    return pl.pallas_call(
        functools.partial(_dsa_kernel, tq=tq, kc=kc, topk=topk),
        grid=(batch, nq),
        in_specs=[pl.BlockSpec((tq, BRANCH_W), lambda b, i: (b * nq + i, UQ_Q_BLK)),
                  pl.BlockSpec((tq, IDX_HEADS * LANES), lambda b, i: (b * nq + i, 0)),
                  pl.BlockSpec((tq, LANES), lambda b, i: (b * nq + i, P32_SMALL_BLK)),
                  kv(P16_IDXK_BLK), kv(P16_DSA_K_BLK), kv(P16_DSA_V_BLK)],
        out_specs=pl.BlockSpec((tq, BRANCH_W), lambda b, i: (b * nq + i, 0)),
        out_shape=jax.ShapeDtypeStruct((t, BRANCH_W), BF16),
        scratch_shapes=[pltpu.VMEM((seq // kc, kc, tq), I32)],
        compiler_params=_cparams(("parallel", "arbitrary")),
        name="dsa_attention",
    )(q_all, q_all, p32, p16, p16, p16)


def _conv_kernel(prev_ref, x_ref, w_ref, o_ref, *, tl):
    first = pl.program_id(1) == 0
    prev = jnp.where(first, 0.0, prev_ref[...])
    ext = jnp.concatenate([prev, x_ref[...]], axis=0)
    acc = ext[SUBLANES:, :] * w_ref[GDN_CONV - 1:GDN_CONV, :]
    for d in range(1, GDN_CONV):
        acc = acc + pltpu.roll(ext, d, 0)[SUBLANES:, :] * w_ref[GDN_CONV - 1 - d:GDN_CONV - d, :]
    y = _silu(acc)
    is_v = pl.program_id(2) == 2
    for h in range(N_HEADS):
        hs = slice(h * HEAD_DIM, (h + 1) * HEAD_DIM)
        yh = y[:, hs]
        nrm = lax.rsqrt(jnp.sum(yh * yh, axis=-1, keepdims=True) + 1e-6)
        o_ref[:, hs] = yh * jnp.where(is_v, 1.0, nrm)


def _gdn_conv(p32, conv_w, batch, seq):
    tl = min(256, seq)
    nl = seq // tl
    t = batch * seq
    per8 = tl // SUBLANES
    return pl.pallas_call(
        functools.partial(_conv_kernel, tl=tl),
        grid=(batch, nl, 3),
        in_specs=[pl.BlockSpec((SUBLANES, BRANCH_W),
                               lambda b, i, c: (jnp.maximum((b * nl + i) * per8 - 1, 0), c)),
                  pl.BlockSpec((tl, BRANCH_W), lambda b, i, c: (b * nl + i, c)),
                  pl.BlockSpec((GDN_CONV, BRANCH_W), lambda b, i, c: (0, c))],
        out_specs=pl.BlockSpec((tl, BRANCH_W), lambda b, i, c: (b * nl + i, c)),
        out_shape=jax.ShapeDtypeStruct((t, 3 * BRANCH_W), F32),
        compiler_params=_cparams(("parallel", "parallel", "arbitrary")),
        name="gdn_conv",
    )(p32, p32, conv_w)


def _gdn_kernel(q_ref, k_ref, v_ref, gate_ref, small_ref, prm_ref, nrm_ref, o_ref, state_ref,
                *, n_chunks):
    c = GDN_CHUNK

    @pl.when(pl.program_id(1) == 0)
    def _():
        state_ref[...] = jnp.zeros_like(state_ref)

    lb = n_chunks * c
    g4 = N_HEADS * c
    ri = lax.broadcasted_iota(I32, (g4, g4), 0)
    cj = lax.broadcasted_iota(I32, (g4, g4), 1)
    sh = c.bit_length() - 1
    same = (ri >> sh) == (cj >> sh)
    incl = same & (ri >= cj)
    strict = same & (ri > cj)
    eye = jnp.where(ri == cj, 1.0, 0.0)
    li = lax.broadcasted_iota(I32, (lb, lb), 0)
    lj = lax.broadcasted_iota(I32, (lb, lb), 1)
    tri_chunks = jnp.where(((li >> sh) == (lj >> sh)) & (li >= lj), 1.0, 0.0).astype(BF16)

    small = small_ref[...]
    beta_all = _sigmoid(small)
    g_all = -jnp.exp(prm_ref[0:1, :]) * _softplus(small + prm_ref[1:2, :])
    gc_all = _dot_left_exact(tri_chunks, g_all)

    def stack(fn):
        return jnp.concatenate([fn(h) for h in range(N_HEADS)], axis=0)

    hsl = lambda h: slice(h * HEAD_DIM, (h + 1) * HEAD_DIM)
    scale = HEAD_DIM ** -0.5
    prepped = []
    for ci in range(n_chunks):
        rs = slice(ci * c, (ci + 1) * c)
        last = slice((ci + 1) * c - 1, (ci + 1) * c)
        k = stack(lambda h: k_ref[rs, hsl(h)])
        q = stack(lambda h: q_ref[rs, hsl(h)]) * scale
        v = stack(lambda h: v_ref[rs, hsl(h)])
        beta = stack(lambda h: beta_all[rs, SMALL_B + h:SMALL_B + h + 1])
        gc = stack(lambda h: gc_all[rs, SMALL_A + h:SMALL_A + h + 1])
        gc_last = stack(lambda h: jnp.broadcast_to(gc_all[last, SMALL_A + h:SMALL_A + h + 1], (c, 1)))
        gc_row = jnp.transpose(jnp.broadcast_to(gc, (g4, LANES)))[0:1, :]
        decay = jnp.where(incl, jnp.exp(jnp.where(incl, gc - gc_row, 0.0)), 0.0)
        k16 = _bf(k)
        k_beta = k * beta
        a = jnp.where(strict, _dot_nt(_bf(k_beta), k16) * decay, 0.0)
        nk = -a
        inv = eye + nk
        for _i in range(5):
            nk16 = _bf(nk)
            nk = _dot(nk16, nk16)
            inv = inv + _dot(_bf(inv), _bf(nk))
        e_gc = jnp.exp(gc)
        sol = _dot_hp(inv, jnp.concatenate([v * beta, k_beta * e_gc], axis=-1))
        qk = jnp.where(incl, _dot_nt(_bf(q), k16) * decay, 0.0)
        prepped.append(dict(u=sol[:, :HEAD_DIM], w16=_bf(sol[:, HEAD_DIM:]), qk16=_bf(qk),
                            qd16=_bf(q * e_gc), kd16=_bf(k * jnp.exp(gc_last - gc)),
                            cdec=[jnp.exp(gc_all[last, SMALL_A + h:SMALL_A + h + 1]) for h in range(N_HEADS)]))

    states = [state_ref[h] for h in range(N_HEADS)]
    for ci, p in enumerate(prepped):
        rs = slice(ci * c, (ci + 1) * c)
        row = lambda h: slice(h * c, (h + 1) * c)
        s16 = [_bf(s) for s in states]
        vn16 = _bf(stack(lambda h: p['u'][row(h)] - _dot(p['w16'][row(h)], s16[h])))
        o = _dot(p['qk16'], vn16) + stack(lambda h: _dot(p['qd16'][row(h)], s16[h]))
        states = [states[h] * p['cdec'][h] + _dot_tn(p['kd16'][row(h)], vn16[row(h)])
                  for h in range(N_HEADS)]
        for h in range(N_HEADS):
            oh = o[row(h)]
            ms = jnp.mean(oh * oh, axis=-1, keepdims=True)
            y = oh * lax.rsqrt(ms + NORM_EPS) * nrm_ref[...]
            o_ref[rs, hsl(h)] = (y * _silu(gate_ref[rs, hsl(h)])).astype(o_ref.dtype)
    for h in range(N_HEADS):
        state_ref[h] = states[h]


def _gated_deltanet(gq, p32, prm, gnorm, batch, seq):
    lb = min(256, seq)
    nb = seq // lb
    t = batch * seq
    blk = lambda cb: pl.BlockSpec((lb, BRANCH_W), lambda b, j, cb=cb: (b * nb + j, cb))
    return pl.pallas_call(
        functools.partial(_gdn_kernel, n_chunks=lb // GDN_CHUNK),
        grid=(batch, nb),
        in_specs=[blk(0), blk(1), blk(2), blk(7),
                  pl.BlockSpec((lb, LANES), lambda b, j: (b * nb + j, P32_SMALL_BLK)),
                  pl.BlockSpec((SUBLANES, LANES), lambda b, j: (0, 0)),
                  pl.BlockSpec((1, HEAD_DIM), lambda b, j: (0, 0))],
        out_specs=pl.BlockSpec((lb, BRANCH_W), lambda b, j: (b * nb + j, 0)),
        out_shape=jax.ShapeDtypeStruct((t, BRANCH_W), BF16),
        scratch_shapes=[pltpu.VMEM((N_HEADS, HEAD_DIM, HEAD_DIM), F32)],
        compiler_params=_cparams(("parallel", "arbitrary")),
        name="gated_deltanet",
    )(gq, gq, gq, p32, p32, prm, gnorm)


def _merge_kernel(x_ref, ya_ref, yb_ref, yc_ref, yd_ref, wg_ref, wb_ref, o_ref):
    x = x_ref[...]
    acc = None
    for i, y_ref in enumerate((ya_ref, yb_ref, yc_ref, yd_ref)):
        term = _sigmoid(_dot(x, wg_ref[i])) * _dot(y_ref[...], wb_ref[i])
        acc = term if acc is None else acc + term
    o_ref[...] = acc.astype(o_ref.dtype)


def _merge(x16, ys, wgl, wbl):
    (wg16, l), (wb16, _) = wgl, wbl
    t = x16.shape[0]
    tm = min(1024, t)
    tn = 256
    yspec = pl.BlockSpec((tm, BRANCH_W), lambda i, j: (i, 0))
    return pl.pallas_call(
        _merge_kernel,
        grid=(t // tm, D_MODEL // tn),
        in_specs=[pl.BlockSpec((tm, D_MODEL), lambda i, j: (i, 0)), yspec, yspec, yspec, yspec,
                  pl.BlockSpec((None, 4, D_MODEL, tn), lambda i, j: (l, 0, 0, j)),
                  pl.BlockSpec((None, 4, BRANCH_W, tn), lambda i, j: (l, 0, 0, j))],
        out_specs=pl.BlockSpec((tm, tn), lambda i, j: (i, j)),
        out_shape=jax.ShapeDtypeStruct((t, D_MODEL), BF16),
        compiler_params=_cparams(("parallel", "arbitrary")),
        name="gated_merge",
    )(x16, *ys, wg16, wb16)


def _layer_norm_rows(h, g, b):
    mu = jnp.mean(h, axis=-1, keepdims=True)
    d = h - mu
    var = jnp.mean(d * d, axis=-1, keepdims=True)
    return d * lax.rsqrt(var + NORM_EPS) * g + b


def _outproj_ln_kernel(m_ref, w_ref, x_ref, g_ref, b_ref, o32_ref, o16_ref):
    h = ALPHA * x_ref[...] + _dot(m_ref[...], w_ref[...])
    y = _layer_norm_rows(h, g_ref[...], b_ref[...])
    o32_ref[...] = y
    o16_ref[...] = _bf(y)


def _outproj_ln(merged16, woutl, x32, g, b):
    w_out16, l = woutl
    t = x32.shape[0]
    tm = min(256, t)
    row = pl.BlockSpec((tm, D_MODEL), lambda i: (i, 0))
    vec = pl.BlockSpec((1, D_MODEL), lambda i: (0, 0))
    return pl.pallas_call(
        _outproj_ln_kernel,
        grid=(t // tm,),
        in_specs=[row, pl.BlockSpec((None, D_MODEL, D_MODEL), lambda i: (l, 0, 0)), row, vec, vec],
        out_specs=[row, row],
        out_shape=[jax.ShapeDtypeStruct((t, D_MODEL), F32), jax.ShapeDtypeStruct((t, D_MODEL), BF16)],
        compiler_params=_cparams(("parallel",)),
        name="outproj_ln",
    )(merged16, w_out16, x32, g, b)


def _router_kernel(x_ref, whi_ref, wlo_ref, b_ref, ids_ref, gates_ref, sizes_ref, run_ref):
    @pl.when(pl.program_id(0) == 0)
    def _():
        run_ref[...] = jnp.zeros_like(run_ref)

    xh, xl = _split2(x_ref[...])
    whi = whi_ref[...]
    logits = _dot(xh, whi) + _dot(xh, wlo_ref[...]) + _dot(xl, whi) + b_ref[...]
    tm = logits.shape[0]
    lane = lax.broadcasted_iota(I32, (tm, LANES), 1)
    ninf = -jnp.inf
    gl = jnp.where(lane < N_GROUPS, logits, ninf)
    gmax = jnp.max(gl, axis=-1, keepdims=True)
    grp = jnp.min(jnp.where(gl == gmax, lane, LANES), axis=-1, keepdims=True)
    p_grp = 1.0 / jnp.sum(jnp.where(lane < N_GROUPS, jnp.exp(logits - gmax), 0.0), axis=-1, keepdims=True)
    lo = N_GROUPS + grp * EXPERTS_PER_GROUP
    el = jnp.where((lane >= lo) & (lane < lo + EXPERTS_PER_GROUP), logits, ninf)
    v1 = jnp.max(el, axis=-1, keepdims=True)
    i1 = jnp.min(jnp.where(el == v1, lane, LANES), axis=-1, keepdims=True)
    el2 = jnp.where(lane == i1, ninf, el)
    v2 = jnp.max(el2, axis=-1, keepdims=True)
    i2 = jnp.min(jnp.where(el2 == v2, lane, LANES), axis=-1, keepdims=True)
    e21 = jnp.exp(v2 - v1)
    den = 1.0 / (1.0 + e21)
    g1 = p_grp * den
    g2 = p_grp * (e21 * den)
    oh = jnp.where(lane == i1, 1.0, jnp.where(lane == i2, 1.0, 0.0))
    tr = lax.broadcasted_iota(I32, (tm, tm), 0)
    tc = lax.broadcasted_iota(I32, (tm, tm), 1)
    earlier = jnp.where(tr > tc, 1.0, 0.0).astype(BF16)
    before = _dot(earlier, _bf(oh)) + run_ref[0:1, :]
    r1 = jnp.sum(jnp.where(lane == i1, before, 0.0), axis=-1, keepdims=True).astype(I32)
    r2 = jnp.sum(jnp.where(lane == i2, before, 0.0), axis=-1, keepdims=True).astype(I32)
    run_new = before[tm - 1:tm, :] + oh[tm - 1:tm, :]
    run_ref[0:1, :] = run_new
    sizes_ref[...] = jnp.broadcast_to(run_new, sizes_ref.shape)
    ids_ref[...] = jnp.where(lane == 0, i1 - N_GROUPS, jnp.where(lane == 1, i2 - N_GROUPS,
                             jnp.where(lane == 2, r1, jnp.where(lane == 3, r2, 0))))
    gates_ref[...] = jnp.where(lane == 0, g1, jnp.where(lane == 1, g2, 0.0))


def _router(x32, whi, wlo, bias):
    t = x32.shape[0]
    tm = min(512, t)
    row = pl.BlockSpec((tm, LANES), lambda i: (i, 0))
    wsp = pl.BlockSpec((D_MODEL, LANES), lambda i: (0, 0))
    return pl.pallas_call(
        _router_kernel,
        grid=(t // tm,),
        in_specs=[pl.BlockSpec((tm, D_MODEL), lambda i: (i, 0)), wsp, wsp,
                  pl.BlockSpec((1, LANES), lambda i: (0, 0))],
        out_specs=[row, row, pl.BlockSpec((SUBLANES, LANES), lambda i: (0, 0))],
        out_shape=[jax.ShapeDtypeStruct((t, LANES), I32), jax.ShapeDtypeStruct((t, LANES), F32),
                   jax.ShapeDtypeStruct((SUBLANES, LANES), F32)],
        scratch_shapes=[pltpu.VMEM((SUBLANES, LANES), F32)],
        compiler_params=_cparams(("arbitrary",)),
        name="moe_router",
    )(x32, whi, wlo, bias)


DMA_UNROLL = 8


def _expert_kernel(be_ref, nb_ref, tok_ref, tok_next_ref, x_hbm, wg_ref, wu_ref, wd_ref, o_ref,
                   xbuf, sem, *, blk):
    i = pl.program_id(0)
    nb = nb_ref[0]
    par = lax.rem(i, 2)

    def row_copy(r, tok, p):
        return pltpu.make_async_copy(x_hbm.at[pl.ds(tok, 1), :], xbuf.at[p, pl.ds(r, 1), :], sem.at[p])

    def issue(tref, p):
        def body(g, _):
            for u in range(DMA_UNROLL):
                r = g * DMA_UNROLL + u
                row_copy(r, tref[0, 0, r], p).start()
            return 0
        lax.fori_loop(0, blk // DMA_UNROLL, body, 0)

    @pl.when(i == 0)
    def _():
        issue(tok_ref, 0)

    @pl.when(i + 1 < nb)
    def _():
        issue(tok_next_ref, 1 - par)

    @pl.when(i < nb)
    def _():
        def wait(g, _):
            for u in range(DMA_UNROLL):
                row_copy(0, 0, par).wait()
            return 0
        lax.fori_loop(0, blk // DMA_UNROLL, wait, 0)
        xe = _bf(xbuf[par])
        h = _silu(_dot(xe, wg_ref[...])) * _dot(xe, wu_ref[...])
        o_ref[...] = _dot(_bf(h), wd_ref[...])

    @pl.when(i >= nb)
    def _():
        o_ref[...] = jnp.zeros_like(o_ref)


def _experts(x32, slot_tok, block_e, nb_used, wgl, wul, wdl, blk):
    (wg16, l), (wu16, _), (wd16, _) = wgl, wul, wdl
    n_blocks = block_e.shape[0]
    wspec = lambda shape: pl.BlockSpec((None, None) + shape, lambda i, be, nb: (l, be[i], 0, 0))
    tok3 = slot_tok.reshape(n_blocks, 1, blk)
    grid_spec = pltpu.PrefetchScalarGridSpec(
        num_scalar_prefetch=2,
        grid=(n_blocks,),
        in_specs=[pl.BlockSpec((1, 1, blk), lambda i, be, nb: (i, 0, 0), memory_space=pltpu.SMEM),
                  pl.BlockSpec((1, 1, blk), lambda i, be, nb: (jnp.minimum(i + 1, n_blocks - 1), 0, 0),
                               memory_space=pltpu.SMEM),
                  pl.BlockSpec(memory_space=pl.ANY),
                  wspec((D_MODEL, D_EXPERT)), wspec((D_MODEL, D_EXPERT)), wspec((D_EXPERT, D_MODEL))],
        out_specs=pl.BlockSpec((blk, D_MODEL), lambda i, be, nb: (i, 0)),
        scratch_shapes=[pltpu.VMEM((2, blk, D_MODEL), F32), pltpu.SemaphoreType.DMA((2,))],
    )
    return pl.pallas_call(
        functools.partial(_expert_kernel, blk=blk),
        grid_spec=grid_spec,
        out_shape=jax.ShapeDtypeStruct((n_blocks * blk, D_MODEL), F32),
        compiler_params=_cparams(("arbitrary",)),
        name="moe_experts",
    )(block_e, nb_used, tok3, tok3, x32, wg16, wu16, wd16)


def _combine_kernel(slot_ref, slot_next_ref, y_hbm, x_ref, gates_ref, g_ref, b_ref, o32_ref, o16_ref,
                    ybuf, sem, *, tm):
    i = pl.program_id(0)
    n = pl.num_programs(0)
    par = lax.rem(i, 2)

    def row_copy(r, k, slot, p):
        return pltpu.make_async_copy(y_hbm.at[pl.ds(slot, 1), :], ybuf.at[p, k, pl.ds(r, 1), :], sem.at[p])

    def issue(sref, p):
        def body(g, _):
            for u in range(DMA_UNROLL):
                r = g * DMA_UNROLL + u
                row_copy(r, 0, sref[0, 0, 2 * r], p).start()
                row_copy(r, 1, sref[0, 0, 2 * r + 1], p).start()
            return 0
        lax.fori_loop(0, tm // DMA_UNROLL, body, 0)

    @pl.when(i == 0)
    def _():
        issue(slot_ref, 0)

    @pl.when(i + 1 < n)
    def _():
        issue(slot_next_ref, 1 - par)

    def wait(g, _):
        for u in range(2 * DMA_UNROLL):
            row_copy(0, 0, 0, par).wait()
        return 0

    lax.fori_loop(0, tm // DMA_UNROLL, wait, 0)
    gates = gates_ref[...]
    y = ybuf[par, 0] * gates[:, 0:1] + ybuf[par, 1] * gates[:, 1:2]
    out = _layer_norm_rows(ALPHA * x_ref[...] + y, g_ref[...], b_ref[...])
    o32_ref[...] = out
    o16_ref[...] = _bf(out)


def _combine_ln(slots, yb, x32, gates, g, b):
    t = x32.shape[0]
    tm = min(256, t)
    n = t // tm
    row = pl.BlockSpec((tm, D_MODEL), lambda i: (i, 0))
    vec = pl.BlockSpec((1, D_MODEL), lambda i: (0, 0))
    slots3 = slots.reshape(n, 1, 2 * tm)
    return pl.pallas_call(
        functools.partial(_combine_kernel, tm=tm),
        grid=(n,),
        in_specs=[pl.BlockSpec((1, 1, 2 * tm), lambda i: (i, 0, 0), memory_space=pltpu.SMEM),
                  pl.BlockSpec((1, 1, 2 * tm), lambda i: (jnp.minimum(i + 1, n - 1), 0, 0),
                               memory_space=pltpu.SMEM),
                  pl.BlockSpec(memory_space=pl.ANY), row,
                  pl.BlockSpec((tm, LANES), lambda i: (i, 0)), vec, vec],
        out_specs=[row, row],
        out_shape=[jax.ShapeDtypeStruct((t, D_MODEL), F32), jax.ShapeDtypeStruct((t, D_MODEL), BF16)],
        scratch_shapes=[pltpu.VMEM((2, 2, tm, D_MODEL), F32), pltpu.SemaphoreType.DMA((2,))],
        compiler_params=_cparams(("arbitrary",)),
        name="moe_combine_ln",
    )(slots3, slots3, yb, x32, gates, g, b)


def _moe_plan(ids, sizes_row, blk):
    t = ids.shape[0]
    sizes = sizes_row[0, N_GROUPS:N_GROUPS + N_EXPERTS].astype(I32)
    padded = (sizes + blk - 1) // blk * blk
    pad_ends = jnp.cumsum(padded)
    pad_starts = pad_ends - padded
    slots = (pad_starts[ids[:, 0:2]] + ids[:, 2:4]).astype(I32)
    n_blocks = 2 * t // blk + N_EXPERTS
    tok_of = jnp.broadcast_to(jnp.arange(t, dtype=I32)[:, None], (t, 2))
    slot_tok = jnp.zeros((n_blocks * blk,), I32).at[slots.reshape(-1)].set(tok_of.reshape(-1))
    starts = jnp.arange(n_blocks, dtype=I32) * blk
    block_e = jnp.minimum(jnp.sum((pad_ends[None, :] <= starts[:, None]).astype(I32), axis=-1),
                          N_EXPERTS - 1).astype(I32)
    nb_used = (pad_ends[-1:] // blk).astype(I32)
    return slots, slot_tok, block_e, nb_used


MOE_BLK = 256
STACKED = ('w16', 'w32', 'wg', 'wb', 'wout', 'eg', 'eu', 'ed')
IN_TM, IN_TN = 1024, 512


def _prep_weights(w_in, dsa_q_norm, w_uq, gdn_conv, gdn_a_log, gdn_dt_bias, gdn_norm, w_branch,
                  w_branch_gate, w_out, ln1_g, ln1_b, w_router_group, b_router_group,
                  w_router_expert, b_router_expert, w_exp_gate, w_exp_up, w_exp_down, ln2_g, ln2_b):
    depth = w_in.shape[0]
    zc = lambda n: jnp.zeros((depth, D_MODEL, n), F32)
    c = lambda a, b: w_in[..., a:b]
    w16 = jnp.concatenate([c(0, 1536), c(3968, 4224), c(4224, 4288), zc(P16_W - 1856)], -1).astype(BF16)
    w32 = jnp.concatenate([c(4296, 5832), c(1536, 3584), c(5840, 6352), c(3584, 3968),
                           c(4288, 4296), c(5832, 5836), c(5836, 5840), zc(LANES - 16)], -1).astype(BF16)
    qi = w_uq[..., BRANCH_W:].reshape(depth, DSA_Q_RANK, IDX_HEADS, IDX_DIM)
    qi = jnp.pad(qi, ((0, 0), (0, 0), (0, 0), (0, LANES - IDX_DIM))).reshape(depth, DSA_Q_RANK, IDX_HEADS * LANES)
    wuq = jnp.concatenate([qi, w_uq[..., :BRANCH_W]], -1).astype(BF16)
    wr = jnp.concatenate([w_router_group, w_router_expert,
                          zc(LANES - N_GROUPS - N_EXPERTS)], -1)
    rhi = wr.astype(BF16)
    rlo = (wr - rhi.astype(F32)).astype(BF16)
    rb = jnp.concatenate([b_router_group, b_router_expert,
                          jnp.zeros((depth, LANES - N_GROUPS - N_EXPERTS), F32)], -1)[:, None, :]
    prm = jnp.zeros((depth, SUBLANES, LANES), F32)
    prm = prm.at[:, 0, SMALL_A:SMALL_A + N_HEADS].set(gdn_a_log)
    prm = prm.at[:, 1, SMALL_A:SMALL_A + N_HEADS].set(gdn_dt_bias)
    return dict(
        w16=w16, w32=w32, q_norm=dsa_q_norm[:, None, :], wuq=wuq, conv=gdn_conv, prm=prm,
        gnorm=gdn_norm[:, None, :], wb=w_branch.astype(BF16), wg=w_branch_gate.astype(BF16),
        wout=w_out.astype(BF16), ln1g=ln1_g[:, None, :], ln1b=ln1_b[:, None, :], rhi=rhi, rlo=rlo, rb=rb,
        eg=w_exp_gate.astype(BF16), eu=w_exp_up.astype(BF16), ed=w_exp_down.astype(BF16),
        ln2g=ln2_g[:, None, :], ln2b=ln2_b[:, None, :])


def _mixer(x16, lw, batch, seq):
    p16 = _matmul(x16, lw['w16'], BF16, IN_TM, IN_TN, "in_proj_bf16")
    p32 = _matmul(x16, lw['w32'], F32, IN_TM, IN_TN, "in_proj_f32")
    y_a = _sb_attention(p16, batch, seq)
    y_b = _retention(p32, batch, seq)
    q_all = _dsa_query(p32, lw['q_norm'], lw['wuq'])
    y_c = _dsa_attention(q_all, p32, p16, batch, seq)
    gq = _gdn_conv(p32, lw['conv'], batch, seq)
    y_d = _gated_deltanet(gq, p32, lw['prm'], lw['gnorm'], batch, seq)
    return _merge(x16, (y_a, y_b, y_c, y_d), lw['wg'], lw['wb'])


def _moe(x32, lw):
    ids, gates, sizes = _router(x32, lw['rhi'], lw['rlo'], lw['rb'])
    slots, slot_tok, block_e, nb_used = _moe_plan(ids, sizes, MOE_BLK)
    yb = _experts(x32, slot_tok, block_e, nb_used, lw['eg'], lw['eu'], lw['ed'], MOE_BLK)
    return _combine_ln(slots, yb, x32, gates, lw['ln2g'], lw['ln2b'])


def _layer_weights(ws, l):
    return {k: ((v, l) if k in STACKED else v[l]) for k, v in ws.items()}


def _layer(x32, x16, lw, batch, seq):
    merged = _mixer(x16, lw, batch, seq)
    x32, x16 = _outproj_ln(merged, lw['wout'], x32, lw['ln1g'], lw['ln1b'])
    return _moe(x32, lw)


def kernel(x, w_in, dsa_q_norm, w_uq, gdn_conv, gdn_a_log, gdn_dt_bias, gdn_norm, w_branch, w_branch_gate,
           w_out, ln1_g, ln1_b, w_router_group, b_router_group, w_router_expert, b_router_expert,
           w_exp_gate, w_exp_up, w_exp_down, ln2_g, ln2_b):
    batch, seq, d = x.shape
    assert d == D_MODEL and seq % RET_CHUNK == 0
    ws = _prep_weights(w_in, dsa_q_norm, w_uq, gdn_conv, gdn_a_log, gdn_dt_bias, gdn_norm, w_branch,
                       w_branch_gate, w_out, ln1_g, ln1_b, w_router_group, b_router_group,
                       w_router_expert, b_router_expert, w_exp_gate, w_exp_up, w_exp_down, ln2_g, ln2_b)
    x32 = x.reshape(batch * seq, d)
    x16 = x32.astype(BF16)
    for l in range(w_in.shape[0]):
        x32, x16 = _layer(x32, x16, _layer_weights(ws, l), batch, seq)
    return x32.reshape(batch, seq, d)
```

```python
import functools
import math

import numpy as np
import jax
import jax.numpy as jnp
from jax import lax
from jax.experimental import pallas as pl
from jax.experimental.pallas import tpu as pltpu

F32 = jnp.float32
BF16 = jnp.bfloat16
I32 = jnp.int32

D_MODEL = 2048
DEPTH = 4
HEAD_DIM = 128
N_HEADS = 4
BRANCH_W = N_HEADS * HEAD_DIM
RET_CHUNK = 128
DSA_Q_RANK = 384
DSA_TOPK = 256
IDX_HEADS = 8
IDX_DIM = 64
GDN_CONV = 4
GDN_CHUNK = 64
N_GROUPS = 4
EXPERTS_PER_GROUP = 8
N_EXPERTS = N_GROUPS * EXPERTS_PER_GROUP
D_EXPERT = 512
ALPHA = (2.0 * DEPTH) ** 0.25
NORM_EPS = 1e-5

LANES = 128
SUBLANES = 8
VMEM_LIMIT_BYTES = 56 * 1024 * 1024

P16_W = 2048
P16_DSA_K_BLK = 12
P16_DSA_V_BLK = 13
P16_IDXK_BLK = 14
P32_W = 4608
P32_SMALL_BLK = 35
SMALL_IDXW, SMALL_B, SMALL_A = 0, 8, 12
UQ_W = IDX_HEADS * LANES + BRANCH_W
UQ_Q_BLK = IDX_HEADS * LANES // BRANCH_W

INT_MIN = -2147483648


def _cparams(sem):
    return pltpu.CompilerParams(dimension_semantics=sem, vmem_limit_bytes=VMEM_LIMIT_BYTES)


def _bf(x):
    return x.astype(BF16)


def _dot(a, b):
    return jnp.dot(a, b, preferred_element_type=F32)


def _dot_nt(a, b):
    return lax.dot_general(a, b, (((1,), (1,)), ((), ())), preferred_element_type=F32)


def _dot_tn(a, b):
    return lax.dot_general(a, b, (((0,), (0,)), ((), ())), preferred_element_type=F32)


def _split2(x):
    hi = _bf(x)
    lo = _bf(x - hi.astype(F32))
    return hi, lo


def _split3(x):
    hi = _bf(x)
    r = x - hi.astype(F32)
    mid = _bf(r)
    lo = _bf(r - mid.astype(F32))
    return hi, mid, lo


def _dot_left_exact(m01, x):
    hi, mid, lo = _split3(x)
    return _dot(m01, hi) + _dot(m01, mid) + _dot(m01, lo)


def _dot_right_exact(x, m01):
    hi, mid, lo = _split3(x)
    return _dot(hi, m01) + _dot(mid, m01) + _dot(lo, m01)


def _dot_hp(a, b):
    ah, al = _split2(a)
    bh, bl = _split2(b)
    return _dot(ah, bh) + _dot(ah, bl) + _dot(al, bh)


def _silu(x):
    return x * (1.0 / (1.0 + jnp.exp(-x)))


def _sigmoid(x):
    return 1.0 / (1.0 + jnp.exp(-x))


def _softplus(x):
    return jnp.maximum(x, 0.0) + jnp.log1p(jnp.exp(-jnp.abs(x)))


def _mm_kernel(x_ref, w_ref, o_ref):
    o_ref[...] = _dot(x_ref[...], w_ref[...]).astype(o_ref.dtype)


def _matmul(x, wl, out_dtype, tm, tn, name):
    w, l = wl
    m, k = x.shape
    n = w.shape[2]
    tm = min(tm, m)
    return pl.pallas_call(
        _mm_kernel,
        grid=(m // tm, n // tn),
        in_specs=[pl.BlockSpec((tm, k), lambda i, j: (i, 0)),
                  pl.BlockSpec((None, k, tn), lambda i, j: (l, 0, j))],
        out_specs=pl.BlockSpec((tm, tn), lambda i, j: (i, j)),
        out_shape=jax.ShapeDtypeStruct((m, n), out_dtype),
        compiler_params=_cparams(("parallel", "arbitrary")),
        name=name,
    )(x, w)


def _sb_kernel(q_ref, k_ref, v_ref, o_ref, *, tq):
    i = pl.program_id(1)
    scale = HEAD_DIM ** -0.5
    hsl = lambda h: slice(h * HEAD_DIM, (h + 1) * HEAD_DIM)
    qs = [q_ref[:, hsl(h)] for h in range(N_HEADS)]
    jr = lax.broadcasted_iota(I32, (tq, tq), 0)
    jc = lax.broadcasted_iota(I32, (tq, tq), 1)
    below = jr > jc
    later = jnp.where(below, 1.0, 0.0).astype(BF16)

    def block(kb, carry, diagonal):
        ks = pl.multiple_of(kb * tq, tq)
        out = []
        for h in range(N_HEADS):
            acc, run = carry[h]
            kblk = k_ref[pl.ds(ks, tq), hsl(h)]
            vblk = v_ref[pl.ds(ks, tq), hsl(h)]
            z = _dot_nt(qs[h], kblk) * scale
            sp = jnp.log(1.0 + jnp.exp(-jnp.abs(z)))
            log_beta = jnp.minimum(z, 0.0) - sp
            log_1m = log_beta - z
            if diagonal:
                log_1m = jnp.where(below, log_1m, 0.0)
            hi, lo = _split2(log_1m)
            suffix = _dot(hi, later) + _dot(lo, later)
            w = jnp.exp(log_beta + suffix + run)
            if diagonal:
                w = jnp.where(below, w, 0.0)
            out.append((acc + _dot(_bf(w), vblk), run + suffix[:, 0:1] + log_1m[:, 0:1]))
        return tuple(out)

    init = tuple((jnp.zeros((tq, HEAD_DIM), F32), jnp.zeros((tq, 1), F32)) for _ in range(N_HEADS))
    res = block(i, init, True)

    def pair(jj, carry):
        kb = i - 1 - 2 * jj
        return block(kb - 1, block(kb, carry, False), False)

    res = lax.fori_loop(0, i // 2, pair, res)
    res = lax.cond(lax.rem(i, 2) == 1, lambda c: block(0, c, False), lambda c: c, res)
    for h in range(N_HEADS):
        o_ref[:, hsl(h)] = res[h][0].astype(o_ref.dtype)


def _sb_attention(p16, batch, seq):
    tq = min(256, seq)
    nq = seq // tq
    t = batch * seq
    return pl.pallas_call(
        functools.partial(_sb_kernel, tq=tq),
        grid=(batch, nq),
        in_specs=[pl.BlockSpec((tq, BRANCH_W), lambda b, i: (b * nq + i, 0)),
                  pl.BlockSpec((seq, BRANCH_W), lambda b, i: (b, 1)),
                  pl.BlockSpec((seq, BRANCH_W), lambda b, i: (b, 2))],
        out_specs=pl.BlockSpec((tq, BRANCH_W), lambda b, i: (b * nq + i, 0)),
        out_shape=jax.ShapeDtypeStruct((t, BRANCH_W), BF16),
        compiler_params=_cparams(("parallel", "arbitrary")),
        name="sb_attention",
    )(p16, p16, p16)


def _ret_kernel(dch_ref, q_ref, k_ref, v_ref, g_ref, cos_ref, sin_ref, dintra_ref, dq_ref, dk_ref,
                o_ref, state_ref, *, n_chunks):
    c = RET_CHUNK

    @pl.when(pl.program_id(1) == 0)
    def _():
        state_ref[...] = jnp.zeros_like(state_ref)

    def rot(x, cs, sn):
        return x * cs + pltpu.roll(x, HEAD_DIM // 2, 1) * sn

    for h in range(N_HEADS):
        hs = slice(h * HEAD_DIM, (h + 1) * HEAD_DIM)
        state = state_ref[h]
        for ci in range(n_chunks):
            rs = slice(ci * c, (ci + 1) * c)
            cs = cos_ref[rs, :]
            sn = sin_ref[rs, :]
            qc = rot(q_ref[rs, hs], cs, sn)
            kc = rot(k_ref[rs, hs], cs, sn) * (HEAD_DIM ** -0.5)
            vc = _bf(v_ref[rs, hs])
            qb = _bf(qc)
            s = _dot_nt(qb, _bf(kc)) * dintra_ref[h]
            o = _dot(_bf(s), vc) + _dot(qb, _bf(state)) * dq_ref[h]
            kv = _dot_tn(_bf(kc * dk_ref[h]), vc)
            state = dch_ref[h] * state + kv
            mu = jnp.mean(o, axis=-1, keepdims=True)
            d = o - mu
            var = jnp.mean(d * d, axis=-1, keepdims=True)
            y = d * lax.rsqrt(var + NORM_EPS)
            o_ref[rs, hs] = (y * _silu(g_ref[rs, hs])).astype(o_ref.dtype)
        state_ref[h] = state


def _retention(p32, batch, seq):
    lb = min(512, seq)
    nb = seq // lb
    t = batch * seq
    f32 = F32
    hh = N_HEADS
    c = RET_CHUNK
    log_gamma = jnp.log1p(-jnp.exp2(-5.0 - jnp.arange(hh, dtype=f32)))
    pos = jnp.arange(c, dtype=f32)
    rel = pos[:, None] - pos[None, :]
    causal = rel >= 0
    d_intra = jnp.where(causal, jnp.exp(jnp.where(causal, rel, 0.0) * log_gamma[:, None, None]), 0.0)
    d_q = jnp.exp((pos + 1.0) * log_gamma[:, None])
    d_k = jnp.exp((c - 1.0 - pos) * log_gamma[:, None])
    d_chunk = jnp.exp(c * log_gamma)
    dq_b = jnp.broadcast_to(d_q[:, :, None], (hh, c, HEAD_DIM))
    dk_b = jnp.broadcast_to(d_k[:, :, None], (hh, c, HEAD_DIM))
    inv = 1.0 / (10000.0 ** (jnp.arange(0, HEAD_DIM, 2, dtype=f32) / HEAD_DIM))
    ang = jnp.arange(seq).astype(f32)[:, None] * inv[None, :]
    cos_t = jnp.concatenate([jnp.cos(ang), jnp.cos(ang)], -1)
    sin_t = jnp.concatenate([-jnp.sin(ang), jnp.sin(ang)], -1)

    qkvg = lambda blk: pl.BlockSpec((lb, BRANCH_W), lambda b, j, blk=blk: (b * nb + j, blk))
    tab = pl.BlockSpec((lb, HEAD_DIM), lambda b, j: (j, 0))
    whole3 = pl.BlockSpec((hh, c, HEAD_DIM), lambda b, j: (0, 0, 0))
    return pl.pallas_call(
        functools.partial(_ret_kernel, n_chunks=lb // c),
        grid=(batch, nb),
        in_specs=[pl.BlockSpec(memory_space=pltpu.SMEM),
                  qkvg(3), qkvg(4), qkvg(5), qkvg(6), tab, tab, whole3, whole3, whole3],
        out_specs=pl.BlockSpec((lb, BRANCH_W), lambda b, j: (b * nb + j, 0)),
        out_shape=jax.ShapeDtypeStruct((t, BRANCH_W), BF16),
        scratch_shapes=[pltpu.VMEM((hh, HEAD_DIM, HEAD_DIM), F32)],
        compiler_params=_cparams(("parallel", "arbitrary")),
        name="retention",
    )(d_chunk, p32, p32, p32, p32, cos_t, sin_t, d_intra, dq_b, dk_b)


def _uq_kernel(x_ref, g_ref, w_ref, o_ref):
    x = x_ref[:, :DSA_Q_RANK]
    ms = jnp.mean(x * x, axis=-1, keepdims=True)
    xn = x * lax.rsqrt(ms + NORM_EPS) * g_ref[...]
    o_ref[...] = _dot(_bf(xn), w_ref[...]).astype(o_ref.dtype)


def _dsa_query(p32, q_norm, w_uq16):
    t = p32.shape[0]
    tm = min(512, t)
    return pl.pallas_call(
        _uq_kernel,
        grid=(t // tm,),
        in_specs=[pl.BlockSpec((tm, 512), lambda i: (i, 8)),
                  pl.BlockSpec((1, DSA_Q_RANK), lambda i: (0, 0)),
                  pl.BlockSpec((DSA_Q_RANK, UQ_W), lambda i: (0, 0))],
        out_specs=pl.BlockSpec((tm, UQ_W), lambda i: (i, 0)),
        out_shape=jax.ShapeDtypeStruct((t, UQ_W), BF16),
        compiler_params=_cparams(("parallel",)),
        name="dsa_query",
    )(p32, q_norm, w_uq16)


def _dsa_kernel(q_ref, qi_ref, small_ref, kidx_ref, k_ref, v_ref, o_ref, keys_ref,
                *, tq, kc, topk):
    i = pl.program_id(1)
    nkv = ((i + 1) * tq + kc - 1) // kc
    qpos = i * tq + lax.broadcasted_iota(I32, (1, tq), 1)
    kofs = lax.broadcasted_iota(I32, (kc, 1), 0)
    w_t = jnp.transpose(small_ref[...] * (IDX_HEADS ** -0.5) * (IDX_DIM ** -0.5))
    wrows = [w_t[SMALL_IDXW + h:SMALL_IDXW + h + 1, :] for h in range(IDX_HEADS)]
    qis = [qi_ref[:, h * LANES:(h + 1) * LANES] for h in range(IDX_HEADS)]

    def score_chunk(j, _):
        ks = pl.multiple_of(j * kc, kc)
        kix = kidx_ref[pl.ds(ks, kc), :]
        acc = jnp.zeros((kc, tq), F32)
        for h in range(IDX_HEADS):
            acc = acc + jnp.maximum(_dot_nt(kix, qis[h]), 0.0) * wrows[h]
        bits = pltpu.bitcast(acc, I32)
        key = jnp.where(bits < 0, bits ^ 0x7FFFFFFF, bits)
        causal = (j * kc + kofs) <= qpos
        keys_ref[j] = jnp.where(causal, key, INT_MIN)
        return 0

    lax.fori_loop(0, nkv, score_chunk, 0)

    def count_ge(cand):
        def cbody(j, part):
            hit = jnp.where(keys_ref[j] >= cand, 1.0, 0.0).reshape(kc // SUBLANES, SUBLANES, tq)
            return part + jnp.sum(hit, axis=0)
        part = lax.fori_loop(0, nkv, cbody, jnp.zeros((SUBLANES, tq), F32))
        return jnp.sum(part, axis=0, keepdims=True)

    def bit_step(it, theta):
        cand = theta + lax.shift_left(jnp.int32(1), 31 - it)
        return jnp.where(count_ge(cand) >= topk, cand, theta)

    theta = lax.fori_loop(0, 32, bit_step, jnp.full((1, tq), INT_MIN, I32))
    n_gt = jnp.where(theta == 2147483647, 0.0, count_ge(theta + 1))
    need = jnp.where(theta == INT_MIN, 0.0, topk - n_gt)

    ur = lax.broadcasted_iota(I32, (kc, kc), 0)
    uc = lax.broadcasted_iota(I32, (kc, kc), 1)
    upto = jnp.where(ur >= uc, 1.0, 0.0).astype(BF16)
    qs = [q_ref[:, h * HEAD_DIM:(h + 1) * HEAD_DIM] for h in range(N_HEADS)]
    scale = HEAD_DIM ** -0.5
    ninf = -jnp.inf

    def attend(j, carry):
        seen, ms, ls, accs = carry
        ks = pl.multiple_of(j * kc, kc)
        key = keys_ref[j]
        eq = key == theta
        pc = _dot(upto, jnp.where(eq, 1.0, 0.0).astype(BF16)) + seen
        rank = jnp.where(eq, pc - need, jnp.where(key > theta, -1.0, 1.0))
        sel = rank <= 0.0
        seen = pc[kc - 1:kc, :]
        kblk = k_ref[pl.ds(ks, kc), :]
        v_t = _bf(jnp.transpose(v_ref[pl.ds(ks, kc), :].astype(F32)))
        ms2, ls2, accs2 = [], [], []
        for h in range(N_HEADS):
            logit = jnp.where(sel, _dot_nt(kblk, qs[h]) * scale, ninf)
            m_new = jnp.maximum(ms[h], jnp.max(logit, axis=0, keepdims=True))
            p = jnp.exp(logit - m_new)
            a = jnp.exp(ms[h] - m_new)
            ls2.append(a * ls[h] + jnp.sum(p, axis=0, keepdims=True))
            accs2.append(a * accs[h] + _dot(v_t, _bf(p)))
            ms2.append(m_new)
        return seen, tuple(ms2), tuple(ls2), tuple(accs2)

    init = (jnp.zeros((1, tq), F32),
            tuple(jnp.full((1, tq), -1e30, F32) for _ in range(N_HEADS)),
            tuple(jnp.zeros((1, tq), F32) for _ in range(N_HEADS)),
            tuple(jnp.zeros((HEAD_DIM, tq), F32) for _ in range(N_HEADS)))
    _, _, ls, accs = lax.fori_loop(0, nkv, attend, init)
    for h in range(N_HEADS):
        o_ref[:, h * HEAD_DIM:(h + 1) * HEAD_DIM] = jnp.transpose(accs[h] / ls[h]).astype(o_ref.dtype)


def _dsa_attention(q_all, p32, p16, batch, seq):
    tq = min(256, seq)
    kc = min(256, seq)
    nq = seq // tq
    t = batch * seq
    topk = min(DSA_TOPK, seq // 4)
    kv = lambda blk: pl.BlockSpec((seq, LANES), lambda b, i, blk=blk: (b, blk))
    return pl.pallas_call(
        functools.partial(_dsa_kernel, tq=tq, kc=kc, topk=topk),
        grid=(batch, nq),
        in_specs=[pl.BlockSpec((tq, BRANCH_W), lambda b, i: (b * nq + i, UQ_Q_BLK)),
                  pl.BlockSpec((tq, IDX_HEADS * LANES), lambda b, i: (b * nq + i, 0)),
                  pl.BlockSpec((tq, LANES), lambda b, i: (b * nq + i, P32_SMALL_BLK)),
                  kv(P16_IDXK_BLK), kv(P16_DSA_K_BLK), kv(P16_DSA_V_BLK)],
        out_specs=pl.BlockSpec((tq, BRANCH_W), lambda b, i: (b * nq + i, 0)),
        out_shape=jax.ShapeDtypeStruct((t, BRANCH_W), BF16),
        scratch_shapes=[pltpu.VMEM((seq // kc, kc, tq), I32)],
        compiler_params=_cparams(("parallel", "arbitrary")),
        name="dsa_attention",
    )(q_all, q_all, p32, p16, p16, p16)


def _conv_kernel(prev_ref, x_ref, w_ref, o_ref, *, tl):
    first = pl.program_id(1) == 0
    prev = jnp.where(first, 0.0, prev_ref[...])
    ext = jnp.concatenate([prev, x_ref[...]], axis=0)
    acc = ext[SUBLANES:, :] * w_ref[GDN_CONV - 1:GDN_CONV, :]
    for d in range(1, GDN_CONV):
        acc = acc + pltpu.roll(ext, d, 0)[SUBLANES:, :] * w_ref[GDN_CONV - 1 - d:GDN_CONV - d, :]
    y = _silu(acc)
    is_v = pl.program_id(2) == 2
    for h in range(N_HEADS):
        hs = slice(h * HEAD_DIM, (h + 1) * HEAD_DIM)
        yh = y[:, hs]
        nrm = lax.rsqrt(jnp.sum(yh * yh, axis=-1, keepdims=True) + 1e-6)
        o_ref[:, hs] = yh * jnp.where(is_v, 1.0, nrm)


def _gdn_conv(p32, conv_w, batch, seq):
    tl = min(256, seq)
    nl = seq // tl
    t = batch * seq
    per8 = tl // SUBLANES
    return pl.pallas_call(
        functools.partial(_conv_kernel, tl=tl),
        grid=(batch, nl, 3),
        in_specs=[pl.BlockSpec((SUBLANES, BRANCH_W),
                               lambda b, i, c: (jnp.maximum((b * nl + i) * per8 - 1, 0), c)),
                  pl.BlockSpec((tl, BRANCH_W), lambda b, i, c: (b * nl + i, c)),
                  pl.BlockSpec((GDN_CONV, BRANCH_W), lambda b, i, c: (0, c))],
        out_specs=pl.BlockSpec((tl, BRANCH_W), lambda b, i, c: (b * nl + i, c)),
        out_shape=jax.ShapeDtypeStruct((t, 3 * BRANCH_W), F32),
        compiler_params=_cparams(("parallel", "parallel", "arbitrary")),
        name="gdn_conv",
    )(p32, p32, conv_w)


def _gdn_kernel(q_ref, k_ref, v_ref, gate_ref, small_ref, prm_ref, nrm_ref, o_ref, state_ref,
                *, n_chunks):
    c = GDN_CHUNK

    @pl.when(pl.program_id(1) == 0)
    def _():
        state_ref[...] = jnp.zeros_like(state_ref)

    lb = n_chunks * c
    g4 = N_HEADS * c
    ri = lax.broadcasted_iota(I32, (g4, g4), 0)
    cj = lax.broadcasted_iota(I32, (g4, g4), 1)
    sh = c.bit_length() - 1
    same = (ri >> sh) == (cj >> sh)
    incl = same & (ri >= cj)
    strict = same & (ri > cj)
    eye = jnp.where(ri == cj, 1.0, 0.0)
    li = lax.broadcasted_iota(I32, (lb, lb), 0)
    lj = lax.broadcasted_iota(I32, (lb, lb), 1)
    tri_chunks = jnp.where(((li >> sh) == (lj >> sh)) & (li >= lj), 1.0, 0.0).astype(BF16)

    small = small_ref[...]
    beta_all = _sigmoid(small)
    g_all = -jnp.exp(prm_ref[0:1, :]) * _softplus(small + prm_ref[1:2, :])
    gc_all = _dot_left_exact(tri_chunks, g_all)

    def stack(fn):
        return jnp.concatenate([fn(h) for h in range(N_HEADS)], axis=0)

    hsl = lambda h: slice(h * HEAD_DIM, (h + 1) * HEAD_DIM)
    scale = HEAD_DIM ** -0.5
    prepped = []
    for ci in range(n_chunks):
        rs = slice(ci * c, (ci + 1) * c)
        last = slice((ci + 1) * c - 1, (ci + 1) * c)
        k = stack(lambda h: k_ref[rs, hsl(h)])
        q = stack(lambda h: q_ref[rs, hsl(h)]) * scale
        v = stack(lambda h: v_ref[rs, hsl(h)])
        beta = stack(lambda h: beta_all[rs, SMALL_B + h:SMALL_B + h + 1])
        gc = stack(lambda h: gc_all[rs, SMALL_A + h:SMALL_A + h + 1])
        gc_last = stack(lambda h: jnp.broadcast_to(gc_all[last, SMALL_A + h:SMALL_A + h + 1], (c, 1)))
        gc_row = jnp.transpose(jnp.broadcast_to(gc, (g4, LANES)))[0:1, :]
        decay = jnp.where(incl, jnp.exp(jnp.where(incl, gc - gc_row, 0.0)), 0.0)
        k16 = _bf(k)
        k_beta = k * beta
        a = jnp.where(strict, _dot_nt(_bf(k_beta), k16) * decay, 0.0)
        nk = -a
        inv = eye + nk
        for _i in range(5):
            nk16 = _bf(nk)
            nk = _dot(nk16, nk16)
            inv = inv + _dot(_bf(inv), _bf(nk))
        e_gc = jnp.exp(gc)
        sol = _dot_hp(inv, jnp.concatenate([v * beta, k_beta * e_gc], axis=-1))
        qk = jnp.where(incl, _dot_nt(_bf(q), k16) * decay, 0.0)
        prepped.append(dict(u=sol[:, :HEAD_DIM], w16=_bf(sol[:, HEAD_DIM:]), qk16=_bf(qk),
                            qd16=_bf(q * e_gc), kd16=_bf(k * jnp.exp(gc_last - gc)),
                            cdec=[jnp.exp(gc_all[last, SMALL_A + h:SMALL_A + h + 1]) for h in range(N_HEADS)]))

    states = [state_ref[h] for h in range(N_HEADS)]
    for ci, p in enumerate(prepped):
        rs = slice(ci * c, (ci + 1) * c)
        row = lambda h: slice(h * c, (h + 1) * c)
        s16 = [_bf(s) for s in states]
        vn16 = _bf(stack(lambda h: p['u'][row(h)] - _dot(p['w16'][row(h)], s16[h])))
        o = _dot(p['qk16'], vn16) + stack(lambda h: _dot(p['qd16'][row(h)], s16[h]))
        states = [states[h] * p['cdec'][h] + _dot_tn(p['kd16'][row(h)], vn16[row(h)])
                  for h in range(N_HEADS)]
        for h in range(N_HEADS):
            oh = o[row(h)]
            ms = jnp.mean(oh * oh, axis=-1, keepdims=True)
            y = oh * lax.rsqrt(ms + NORM_EPS) * nrm_ref[...]
            o_ref[rs, hsl(h)] = (y * _silu(gate_ref[rs, hsl(h)])).astype(o_ref.dtype)
    for h in range(N_HEADS):
        state_ref[h] = states[h]


def _gated_deltanet(gq, p32, prm, gnorm, batch, seq):
    lb = min(256, seq)
    nb = seq // lb
    t = batch * seq
    blk = lambda cb: pl.BlockSpec((lb, BRANCH_W), lambda b, j, cb=cb: (b * nb + j, cb))
    return pl.pallas_call(
        functools.partial(_gdn_kernel, n_chunks=lb // GDN_CHUNK),
        grid=(batch, nb),
        in_specs=[blk(0), blk(1), blk(2), blk(7),
                  pl.BlockSpec((lb, LANES), lambda b, j: (b * nb + j, P32_SMALL_BLK)),
                  pl.BlockSpec((SUBLANES, LANES), lambda b, j: (0, 0)),
                  pl.BlockSpec((1, HEAD_DIM), lambda b, j: (0, 0))],
        out_specs=pl.BlockSpec((lb, BRANCH_W), lambda b, j: (b * nb + j, 0)),
        out_shape=jax.ShapeDtypeStruct((t, BRANCH_W), BF16),
        scratch_shapes=[pltpu.VMEM((N_HEADS, HEAD_DIM, HEAD_DIM), F32)],
        compiler_params=_cparams(("parallel", "arbitrary")),
        name="gated_deltanet",
    )(gq, gq, gq, p32, p32, prm, gnorm)


def _merge_kernel(x_ref, ya_ref, yb_ref, yc_ref, yd_ref, wg_ref, wb_ref, o_ref):
    x = x_ref[...]
    acc = None
    for i, y_ref in enumerate((ya_ref, yb_ref, yc_ref, yd_ref)):
        term = _sigmoid(_dot(x, wg_ref[i])) * _dot(y_ref[...], wb_ref[i])
        acc = term if acc is None else acc + term
    o_ref[...] = acc.astype(o_ref.dtype)


def _merge(x16, ys, wgl, wbl):
    (wg16, l), (wb16, _) = wgl, wbl
    t = x16.shape[0]
    tm = min(1024, t)
    tn = 256
    yspec = pl.BlockSpec((tm, BRANCH_W), lambda i, j: (i, 0))
    return pl.pallas_call(
        _merge_kernel,
        grid=(t // tm, D_MODEL // tn),
        in_specs=[pl.BlockSpec((tm, D_MODEL), lambda i, j: (i, 0)), yspec, yspec, yspec, yspec,
                  pl.BlockSpec((None, 4, D_MODEL, tn), lambda i, j: (l, 0, 0, j)),
                  pl.BlockSpec((None, 4, BRANCH_W, tn), lambda i, j: (l, 0, 0, j))],
        out_specs=pl.BlockSpec((tm, tn), lambda i, j: (i, j)),
        out_shape=jax.ShapeDtypeStruct((t, D_MODEL), BF16),
        compiler_params=_cparams(("parallel", "arbitrary")),
        name="gated_merge",
    )(x16, *ys, wg16, wb16)


def _layer_norm_rows(h, g, b):
    mu = jnp.mean(h, axis=-1, keepdims=True)
    d = h - mu
    var = jnp.mean(d * d, axis=-1, keepdims=True)
    return d * lax.rsqrt(var + NORM_EPS) * g + b


def _outproj_ln_kernel(m_ref, w_ref, x_ref, g_ref, b_ref, o32_ref, o16_ref):
    h = ALPHA * x_ref[...] + _dot(m_ref[...], w_ref[...])
    y = _layer_norm_rows(h, g_ref[...], b_ref[...])
    o32_ref[...] = y
    o16_ref[...] = _bf(y)


def _outproj_ln(merged16, woutl, x32, g, b):
    w_out16, l = woutl
    t = x32.shape[0]
    tm = min(256, t)
    row = pl.BlockSpec((tm, D_MODEL), lambda i: (i, 0))
    vec = pl.BlockSpec((1, D_MODEL), lambda i: (0, 0))
    return pl.pallas_call(
        _outproj_ln_kernel,
        grid=(t // tm,),
        in_specs=[row, pl.BlockSpec((None, D_MODEL, D_MODEL), lambda i: (l, 0, 0)), row, vec, vec],
        out_specs=[row, row],
        out_shape=[jax.ShapeDtypeStruct((t, D_MODEL), F32), jax.ShapeDtypeStruct((t, D_MODEL), BF16)],
        compiler_params=_cparams(("parallel",)),
        name="outproj_ln",
    )(merged16, w_out16, x32, g, b)


def _router_kernel(x_ref, whi_ref, wlo_ref, b_ref, ids_ref, gates_ref, sizes_ref, run_ref):
    @pl.when(pl.program_id(0) == 0)
    def _():
        run_ref[...] = jnp.zeros_like(run_ref)

    xh, xl = _split2(x_ref[...])
    whi = whi_ref[...]
    logits = _dot(xh, whi) + _dot(xh, wlo_ref[...]) + _dot(xl, whi) + b_ref[...]
    tm = logits.shape[0]
    lane = lax.broadcasted_iota(I32, (tm, LANES), 1)
    ninf = -jnp.inf
    gl = jnp.where(lane < N_GROUPS, logits, ninf)
    gmax = jnp.max(gl, axis=-1, keepdims=True)
    grp = jnp.min(jnp.where(gl == gmax, lane, LANES), axis=-1, keepdims=True)
    p_grp = 1.0 / jnp.sum(jnp.where(lane < N_GROUPS, jnp.exp(logits - gmax), 0.0), axis=-1, keepdims=True)
    lo = N_GROUPS + grp * EXPERTS_PER_GROUP
    el = jnp.where((lane >= lo) & (lane < lo + EXPERTS_PER_GROUP), logits, ninf)
    v1 = jnp.max(el, axis=-1, keepdims=True)
    i1 = jnp.min(jnp.where(el == v1, lane, LANES), axis=-1, keepdims=True)
    el2 = jnp.where(lane == i1, ninf, el)
    v2 = jnp.max(el2, axis=-1, keepdims=True)
    i2 = jnp.min(jnp.where(el2 == v2, lane, LANES), axis=-1, keepdims=True)
    e21 = jnp.exp(v2 - v1)
    den = 1.0 / (1.0 + e21)
    g1 = p_grp * den
    g2 = p_grp * (e21 * den)
    oh = jnp.where(lane == i1, 1.0, jnp.where(lane == i2, 1.0, 0.0))
    tr = lax.broadcasted_iota(I32, (tm, tm), 0)
    tc = lax.broadcasted_iota(I32, (tm, tm), 1)
    earlier = jnp.where(tr > tc, 1.0, 0.0).astype(BF16)
    before = _dot(earlier, _bf(oh)) + run_ref[0:1, :]
    r1 = jnp.sum(jnp.where(lane == i1, before, 0.0), axis=-1, keepdims=True).astype(I32)
    r2 = jnp.sum(jnp.where(lane == i2, before, 0.0), axis=-1, keepdims=True).astype(I32)
    run_new = before[tm - 1:tm, :] + oh[tm - 1:tm, :]
    run_ref[0:1, :] = run_new
    sizes_ref[...] = jnp.broadcast_to(run_new, sizes_ref.shape)
    ids_ref[...] = jnp.where(lane == 0, i1 - N_GROUPS, jnp.where(lane == 1, i2 - N_GROUPS,
                             jnp.where(lane == 2, r1, jnp.where(lane == 3, r2, 0))))
    gates_ref[...] = jnp.where(lane == 0, g1, jnp.where(lane == 1, g2, 0.0))


def _router(x32, whi, wlo, bias):
    t = x32.shape[0]
    tm = min(512, t)
    row = pl.BlockSpec((tm, LANES), lambda i: (i, 0))
    wsp = pl.BlockSpec((D_MODEL, LANES), lambda i: (0, 0))
    return pl.pallas_call(
        _router_kernel,
        grid=(t // tm,),
        in_specs=[pl.BlockSpec((tm, D_MODEL), lambda i: (i, 0)), wsp, wsp,
                  pl.BlockSpec((1, LANES), lambda i: (0, 0))],
        out_specs=[row, row, pl.BlockSpec((SUBLANES, LANES), lambda i: (0, 0))],
        out_shape=[jax.ShapeDtypeStruct((t, LANES), I32), jax.ShapeDtypeStruct((t, LANES), F32),
                   jax.ShapeDtypeStruct((SUBLANES, LANES), F32)],
        scratch_shapes=[pltpu.VMEM((SUBLANES, LANES), F32)],
        compiler_params=_cparams(("arbitrary",)),
        name="moe_router",
    )(x32, whi, wlo, bias)


DMA_UNROLL = 8


def _expert_kernel(be_ref, nb_ref, tok_ref, tok_next_ref, x_hbm, wg_ref, wu_ref, wd_ref, o_ref,
                   xbuf, wg16, wu16, wd16, sem, *, blk):
    i = pl.program_id(0)
    nb = nb_ref[0]
    par = lax.rem(i, 2)

    def row_copy(r, tok, p):
        return pltpu.make_async_copy(x_hbm.at[pl.ds(tok, 1), :], xbuf.at[p, pl.ds(r, 1), :], sem.at[p])

    def issue(tref, p):
        def body(g, _):
            for u in range(DMA_UNROLL):
                r = g * DMA_UNROLL + u
                row_copy(r, tref[0, 0, r], p).start(priority=u % 2)
            return 0
        lax.fori_loop(0, blk // DMA_UNROLL, body, 0)

    @pl.when((i == 0) | (be_ref[i] != be_ref[jnp.maximum(i - 1, 0)]))
    def _():
        wg16[...] = _bf(wg_ref[...])
        wu16[...] = _bf(wu_ref[...])
        wd16[...] = _bf(wd_ref[...])

    @pl.when(i == 0)
    def _():
        issue(tok_ref, 0)

    @pl.when(i + 1 < nb)
    def _():
        issue(tok_next_ref, 1 - par)

    @pl.when(i < nb)
    def _():
        def wait(g, _):
            for u in range(DMA_UNROLL):
                row_copy(0, 0, par).wait()
            return 0
        lax.fori_loop(0, blk // DMA_UNROLL, wait, 0)
        xe = _bf(xbuf[par])
        h = _silu(_dot(xe, wg16[...])) * _dot(xe, wu16[...])
        o_ref[...] = _dot(_bf(h), wd16[...])

    @pl.when(i >= nb)
    def _():
        o_ref[...] = jnp.zeros_like(o_ref)


def _experts(x32, slot_tok, block_e, nb_used, wgl, wul, wdl, blk):
    (wg16, l), (wu16, _), (wd16, _) = wgl, wul, wdl
    n_blocks = block_e.shape[0]
    wspec = lambda shape: pl.BlockSpec((None, None) + shape, lambda i, be, nb: (l, be[i], 0, 0))
    tok3 = slot_tok.reshape(n_blocks, 1, blk)
    grid_spec = pltpu.PrefetchScalarGridSpec(
        num_scalar_prefetch=2,
        grid=(n_blocks,),
        in_specs=[pl.BlockSpec((1, 1, blk), lambda i, be, nb: (i, 0, 0), memory_space=pltpu.SMEM),
                  pl.BlockSpec((1, 1, blk), lambda i, be, nb: (jnp.minimum(i + 1, n_blocks - 1), 0, 0),
                               memory_space=pltpu.SMEM),
                  pl.BlockSpec(memory_space=pl.ANY),
                  wspec((D_MODEL, D_EXPERT)), wspec((D_MODEL, D_EXPERT)), wspec((D_EXPERT, D_MODEL))],
        out_specs=pl.BlockSpec((blk, D_MODEL), lambda i, be, nb: (i, 0)),
        scratch_shapes=[pltpu.VMEM((2, blk, D_MODEL), F32),
                        pltpu.VMEM((D_MODEL, D_EXPERT), BF16), pltpu.VMEM((D_MODEL, D_EXPERT), BF16),
                        pltpu.VMEM((D_EXPERT, D_MODEL), BF16), pltpu.SemaphoreType.DMA((2,))],
    )
    return pl.pallas_call(
        functools.partial(_expert_kernel, blk=blk),
        grid_spec=grid_spec,
        out_shape=jax.ShapeDtypeStruct((n_blocks * blk, D_MODEL), F32),
        compiler_params=_cparams(("arbitrary",)),
        name="moe_experts",
    )(block_e, nb_used, tok3, tok3, x32, wg16, wu16, wd16)


def _combine_kernel(slot_ref, slot_next_ref, y_hbm, x_ref, gates_ref, g_ref, b_ref, o32_ref, o16_ref,
                    ybuf, sem, *, tm):
    i = pl.program_id(0)
    n = pl.num_programs(0)
    par = lax.rem(i, 2)

    def row_copy(r, k, slot, p):
        return pltpu.make_async_copy(y_hbm.at[pl.ds(slot, 1), :], ybuf.at[p, k, pl.ds(r, 1), :], sem.at[p])

    def issue(sref, p):
        def body(g, _):
            for u in range(DMA_UNROLL):
                r = g * DMA_UNROLL + u
                row_copy(r, 0, sref[0, 0, 2 * r], p).start(priority=0)
                row_copy(r, 1, sref[0, 0, 2 * r + 1], p).start(priority=1)
            return 0
        lax.fori_loop(0, tm // DMA_UNROLL, body, 0)

    @pl.when(i == 0)
    def _():
        issue(slot_ref, 0)

    @pl.when(i + 1 < n)
    def _():
        issue(slot_next_ref, 1 - par)

    def wait(g, _):
        for u in range(2 * DMA_UNROLL):
            row_copy(0, 0, 0, par).wait()
        return 0

    lax.fori_loop(0, tm // DMA_UNROLL, wait, 0)
    gates = gates_ref[...]
    y = ybuf[par, 0] * gates[:, 0:1] + ybuf[par, 1] * gates[:, 1:2]
    out = _layer_norm_rows(ALPHA * x_ref[...] + y, g_ref[...], b_ref[...])
    o32_ref[...] = out
    o16_ref[...] = _bf(out)


def _combine_ln(slots, yb, x32, gates, g, b):
    t = x32.shape[0]
    tm = min(256, t)
    n = t // tm
    row = pl.BlockSpec((tm, D_MODEL), lambda i: (i, 0))
    vec = pl.BlockSpec((1, D_MODEL), lambda i: (0, 0))
    slots3 = slots.reshape(n, 1, 2 * tm)
    return pl.pallas_call(
        functools.partial(_combine_kernel, tm=tm),
        grid=(n,),
        in_specs=[pl.BlockSpec((1, 1, 2 * tm), lambda i: (i, 0, 0), memory_space=pltpu.SMEM),
                  pl.BlockSpec((1, 1, 2 * tm), lambda i: (jnp.minimum(i + 1, n - 1), 0, 0),
                               memory_space=pltpu.SMEM),
                  pl.BlockSpec(memory_space=pl.ANY), row,
                  pl.BlockSpec((tm, LANES), lambda i: (i, 0)), vec, vec],
        out_specs=[row, row],
        out_shape=[jax.ShapeDtypeStruct((t, D_MODEL), F32), jax.ShapeDtypeStruct((t, D_MODEL), BF16)],
        scratch_shapes=[pltpu.VMEM((2, 2, tm, D_MODEL), F32), pltpu.SemaphoreType.DMA((2,))],
        compiler_params=_cparams(("arbitrary",)),
        name="moe_combine_ln",
    )(slots3, slots3, yb, x32, gates, g, b)


def _moe_plan(ids, sizes_row, blk):
    t = ids.shape[0]
    sizes = sizes_row[0, N_GROUPS:N_GROUPS + N_EXPERTS].astype(I32)
    padded = (sizes + blk - 1) // blk * blk
    pad_ends = jnp.cumsum(padded)
    pad_starts = pad_ends - padded
    slots = (pad_starts[ids[:, 0:2]] + ids[:, 2:4]).astype(I32)
    n_blocks = 2 * t // blk + N_EXPERTS
    tok_of = jnp.broadcast_to(jnp.arange(t, dtype=I32)[:, None], (t, 2))
    slot_tok = jnp.zeros((n_blocks * blk,), I32).at[slots.reshape(-1)].set(tok_of.reshape(-1))
    starts = jnp.arange(n_blocks, dtype=I32) * blk
    block_e = jnp.minimum(jnp.sum((pad_ends[None, :] <= starts[:, None]).astype(I32), axis=-1),
                          N_EXPERTS - 1).astype(I32)
    nb_used = (pad_ends[-1:] // blk).astype(I32)
    return slots, slot_tok, block_e, nb_used


MOE_BLK = 256
STACKED = ('w16', 'w32', 'wg', 'wb', 'wout', 'eg', 'eu', 'ed')
IN_TM, IN_TN = 1024, 512


def _prep_weights(w_in, dsa_q_norm, w_uq, gdn_conv, gdn_a_log, gdn_dt_bias, gdn_norm, w_branch,
                  w_branch_gate, w_out, ln1_g, ln1_b, w_router_group, b_router_group,
                  w_router_expert, b_router_expert, w_exp_gate, w_exp_up, w_exp_down, ln2_g, ln2_b):
    depth = w_in.shape[0]
    zc = lambda n: jnp.zeros((depth, D_MODEL, n), F32)
    c = lambda a, b: w_in[..., a:b]
    w16 = jnp.concatenate([c(0, 1536), c(3968, 4224), c(4224, 4288), zc(P16_W - 1856)], -1).astype(BF16)
    w32 = jnp.concatenate([c(4296, 5832), c(1536, 3584), c(5840, 6352), c(3584, 3968),
                           c(4288, 4296), c(5832, 5836), c(5836, 5840), zc(LANES - 16)], -1).astype(BF16)
    qi = w_uq[..., BRANCH_W:].reshape(depth, DSA_Q_RANK, IDX_HEADS, IDX_DIM)
    qi = jnp.pad(qi, ((0, 0), (0, 0), (0, 0), (0, LANES - IDX_DIM))).reshape(depth, DSA_Q_RANK, IDX_HEADS * LANES)
    wuq = jnp.concatenate([qi, w_uq[..., :BRANCH_W]], -1).astype(BF16)
    wr = jnp.concatenate([w_router_group, w_router_expert,
                          zc(LANES - N_GROUPS - N_EXPERTS)], -1)
    rhi = wr.astype(BF16)
    rlo = (wr - rhi.astype(F32)).astype(BF16)
    rb = jnp.concatenate([b_router_group, b_router_expert,
                          jnp.zeros((depth, LANES - N_GROUPS - N_EXPERTS), F32)], -1)[:, None, :]
    prm = jnp.zeros((depth, SUBLANES, LANES), F32)
    prm = prm.at[:, 0, SMALL_A:SMALL_A + N_HEADS].set(gdn_a_log)
    prm = prm.at[:, 1, SMALL_A:SMALL_A + N_HEADS].set(gdn_dt_bias)
    return dict(
        w16=w16, w32=w32, q_norm=dsa_q_norm[:, None, :], wuq=wuq, conv=gdn_conv, prm=prm,
        gnorm=gdn_norm[:, None, :], wb=w_branch.astype(BF16), wg=w_branch_gate.astype(BF16),
        wout=w_out.astype(BF16), ln1g=ln1_g[:, None, :], ln1b=ln1_b[:, None, :], rhi=rhi, rlo=rlo, rb=rb,
        eg=w_exp_gate, eu=w_exp_up, ed=w_exp_down,
        ln2g=ln2_g[:, None, :], ln2b=ln2_b[:, None, :])


def _mixer(x16, lw, batch, seq):
    p16 = _matmul(x16, lw['w16'], BF16, IN_TM, IN_TN, "in_proj_bf16")
    p32 = _matmul(x16, lw['w32'], F32, IN_TM, IN_TN, "in_proj_f32")
    y_a = _sb_attention(p16, batch, seq)
    y_b = _retention(p32, batch, seq)
    q_all = _dsa_query(p32, lw['q_norm'], lw['wuq'])
    y_c = _dsa_attention(q_all, p32, p16, batch, seq)
    gq = _gdn_conv(p32, lw['conv'], batch, seq)
    y_d = _gated_deltanet(gq, p32, lw['prm'], lw['gnorm'], batch, seq)
    return _merge(x16, (y_a, y_b, y_c, y_d), lw['wg'], lw['wb'])


def _moe(x32, lw):
    ids, gates, sizes = _router(x32, lw['rhi'], lw['rlo'], lw['rb'])
    slots, slot_tok, block_e, nb_used = _moe_plan(ids, sizes, MOE_BLK)
    yb = _experts(x32, slot_tok, block_e, nb_used, lw['eg'], lw['eu'], lw['ed'], MOE_BLK)
    return _combine_ln(slots, yb, x32, gates, lw['ln2g'], lw['ln2b'])


def _layer_weights(ws, l):
    return {k: ((v, l) if k in STACKED else v[l]) for k, v in ws.items()}


def _layer(x32, x16, lw, batch, seq):
    merged = _mixer(x16, lw, batch, seq)
    x32, x16 = _outproj_ln(merged, lw['wout'], x32, lw['ln1g'], lw['ln1b'])
    return _moe(x32, lw)


def kernel(x, w_in, dsa_q_norm, w_uq, gdn_conv, gdn_a_log, gdn_dt_bias, gdn_norm, w_branch, w_branch_gate,
           w_out, ln1_g, ln1_b, w_router_group, b_router_group, w_router_expert, b_router_expert,
           w_exp_gate, w_exp_up, w_exp_down, ln2_g, ln2_b):
    batch, seq, d = x.shape
    assert d == D_MODEL and seq % RET_CHUNK == 0
    ws = _prep_weights(w_in, dsa_q_norm, w_uq, gdn_conv, gdn_a_log, gdn_dt_bias, gdn_norm, w_branch,
                       w_branch_gate, w_out, ln1_g, ln1_b, w_router_group, b_router_group,
                       w_router_expert, b_router_expert, w_exp_gate, w_exp_up, w_exp_down, ln2_g, ln2_b)
    x32 = x.reshape(batch * seq, d)
    x16 = x32.astype(BF16)
    for l in range(w_in.shape[0]):
        x32, x16 = _layer(x32, x16, _layer_weights(ws, l), batch, seq)
    return x32.reshape(batch, seq, d)
```

```python
import functools
import math

import numpy as np
import jax
import jax.numpy as jnp
from jax import lax
from jax.experimental import pallas as pl
from jax.experimental.pallas import tpu as pltpu

F32 = jnp.float32
BF16 = jnp.bfloat16
I32 = jnp.int32

D_MODEL = 2048
DEPTH = 4
HEAD_DIM = 128
N_HEADS = 4
BRANCH_W = N_HEADS * HEAD_DIM
RET_CHUNK = 128
DSA_Q_RANK = 384
DSA_TOPK = 256
IDX_HEADS = 8
IDX_DIM = 64
GDN_CONV = 4
GDN_CHUNK = 64
N_GROUPS = 4
EXPERTS_PER_GROUP = 8
N_EXPERTS = N_GROUPS * EXPERTS_PER_GROUP
D_EXPERT = 512
ALPHA = (2.0 * DEPTH) ** 0.25
NORM_EPS = 1e-5

LANES = 128
SUBLANES = 8
VMEM_LIMIT_BYTES = 56 * 1024 * 1024

P16_W = 2048
P16_DSA_K_BLK = 12
P16_DSA_V_BLK = 13
P16_IDXK_BLK = 14
P32_W = 4608
P32_SMALL_BLK = 35
SMALL_IDXW, SMALL_B, SMALL_A = 0, 8, 12
UQ_W = IDX_HEADS * LANES + BRANCH_W
UQ_Q_BLK = IDX_HEADS * LANES // BRANCH_W

INT_MIN = -2147483648


def _cparams(sem):
    return pltpu.CompilerParams(dimension_semantics=sem, vmem_limit_bytes=VMEM_LIMIT_BYTES)


def _bf(x):
    return x.astype(BF16)


def _dot(a, b):
    return jnp.dot(a, b, preferred_element_type=F32)


def _dot_nt(a, b):
    return lax.dot_general(a, b, (((1,), (1,)), ((), ())), preferred_element_type=F32)


def _dot_tn(a, b):
    return lax.dot_general(a, b, (((0,), (0,)), ((), ())), preferred_element_type=F32)


def _split2(x):
    hi = _bf(x)
    lo = _bf(x - hi.astype(F32))
    return hi, lo


def _split3(x):
    hi = _bf(x)
    r = x - hi.astype(F32)
    mid = _bf(r)
    lo = _bf(r - mid.astype(F32))
    return hi, mid, lo


def _dot_left_exact(m01, x):
    hi, mid, lo = _split3(x)
    return _dot(m01, hi) + _dot(m01, mid) + _dot(m01, lo)


def _dot_right_exact(x, m01):
    hi, mid, lo = _split3(x)
    return _dot(hi, m01) + _dot(mid, m01) + _dot(lo, m01)


def _dot_hp(a, b):
    ah, al = _split2(a)
    bh, bl = _split2(b)
    return _dot(ah, bh) + _dot(ah, bl) + _dot(al, bh)


def _silu(x):
    return x * (1.0 / (1.0 + jnp.exp(-x)))


def _sigmoid(x):
    return 1.0 / (1.0 + jnp.exp(-x))


def _softplus(x):
    return jnp.maximum(x, 0.0) + jnp.log1p(jnp.exp(-jnp.abs(x)))


def _mm_kernel(x_ref, w_ref, o_ref):
    o_ref[...] = _dot(x_ref[...], w_ref[...]).astype(o_ref.dtype)


def _matmul(x, wl, out_dtype, tm, tn, name):
    w, l = wl
    m, k = x.shape
    n = w.shape[2]
    tm = min(tm, m)
    return pl.pallas_call(
        _mm_kernel,
        grid=(m // tm, n // tn),
        in_specs=[pl.BlockSpec((tm, k), lambda i, j: (i, 0)),
                  pl.BlockSpec((None, k, tn), lambda i, j: (l, 0, j))],
        out_specs=pl.BlockSpec((tm, tn), lambda i, j: (i, j)),
        out_shape=jax.ShapeDtypeStruct((m, n), out_dtype),
        compiler_params=_cparams(("parallel", "arbitrary")),
        name=name,
    )(x, w)


def _sb_kernel(q_ref, k_ref, v_ref, o_ref, *, tq):
    i = pl.program_id(1)
    scale = HEAD_DIM ** -0.5
    hsl = lambda h: slice(h * HEAD_DIM, (h + 1) * HEAD_DIM)
    qs = [q_ref[:, hsl(h)] for h in range(N_HEADS)]
    jr = lax.broadcasted_iota(I32, (tq, tq), 0)
    jc = lax.broadcasted_iota(I32, (tq, tq), 1)
    below = jr > jc
    later = jnp.where(below, 1.0, 0.0).astype(BF16)

    def block(kb, carry, diagonal):
        ks = pl.multiple_of(kb * tq, tq)
        out = []
        for h in range(N_HEADS):
            acc, run = carry[h]
            kblk = k_ref[pl.ds(ks, tq), hsl(h)]
            vblk = v_ref[pl.ds(ks, tq), hsl(h)]
            z = _dot_nt(qs[h], kblk) * scale
            sp = jnp.log(1.0 + jnp.exp(-jnp.abs(z)))
            log_beta = jnp.minimum(z, 0.0) - sp
            log_1m = log_beta - z
            if diagonal:
                log_1m = jnp.where(below, log_1m, 0.0)
            hi, lo = _split2(log_1m)
            suffix = _dot(hi, later) + _dot(lo, later)
            w = jnp.exp(log_beta + suffix + run)
            if diagonal:
                w = jnp.where(below, w, 0.0)
            out.append((acc + _dot(_bf(w), vblk), run + suffix[:, 0:1] + log_1m[:, 0:1]))
        return tuple(out)

    init = tuple((jnp.zeros((tq, HEAD_DIM), F32), jnp.zeros((tq, 1), F32)) for _ in range(N_HEADS))
    res = block(i, init, True)

    def pair(jj, carry):
        kb = i - 1 - 2 * jj
        return block(kb - 1, block(kb, carry, False), False)

    res = lax.fori_loop(0, i // 2, pair, res)
    res = lax.cond(lax.rem(i, 2) == 1, lambda c: block(0, c, False), lambda c: c, res)
    for h in range(N_HEADS):
        o_ref[:, hsl(h)] = res[h][0].astype(o_ref.dtype)


def _sb_attention(p16, batch, seq):
    tq = min(256, seq)
    nq = seq // tq
    t = batch * seq
    return pl.pallas_call(
        functools.partial(_sb_kernel, tq=tq),
        grid=(batch, nq),
        in_specs=[pl.BlockSpec((tq, BRANCH_W), lambda b, i: (b * nq + i, 0)),
                  pl.BlockSpec((seq, BRANCH_W), lambda b, i: (b, 1)),
                  pl.BlockSpec((seq, BRANCH_W), lambda b, i: (b, 2))],
        out_specs=pl.BlockSpec((tq, BRANCH_W), lambda b, i: (b * nq + i, 0)),
        out_shape=jax.ShapeDtypeStruct((t, BRANCH_W), BF16),
        compiler_params=_cparams(("parallel", "arbitrary")),
        name="sb_attention",
    )(p16, p16, p16)


def _ret_kernel(dch_ref, q_ref, k_ref, v_ref, g_ref, cos_ref, sin_ref, dintra_ref, dq_ref, dk_ref,
                o_ref, state_ref, *, n_chunks):
    c = RET_CHUNK

    @pl.when(pl.program_id(1) == 0)
    def _():
        state_ref[...] = jnp.zeros_like(state_ref)

    def rot(x, cs, sn):
        return x * cs + pltpu.roll(x, HEAD_DIM // 2, 1) * sn

    for h in range(N_HEADS):
        hs = slice(h * HEAD_DIM, (h + 1) * HEAD_DIM)
        state = state_ref[h]
        for ci in range(n_chunks):
            rs = slice(ci * c, (ci + 1) * c)
            cs = cos_ref[rs, :]
            sn = sin_ref[rs, :]
            qc = rot(q_ref[rs, hs], cs, sn)
            kc = rot(k_ref[rs, hs], cs, sn) * (HEAD_DIM ** -0.5)
            vc = _bf(v_ref[rs, hs])
            qb = _bf(qc)
            s = _dot_nt(qb, _bf(kc)) * dintra_ref[h]
            o = _dot(_bf(s), vc) + _dot(qb, _bf(state)) * dq_ref[h]
            kv = _dot_tn(_bf(kc * dk_ref[h]), vc)
            state = dch_ref[h] * state + kv
            mu = jnp.mean(o, axis=-1, keepdims=True)
            d = o - mu
            var = jnp.mean(d * d, axis=-1, keepdims=True)
            y = d * lax.rsqrt(var + NORM_EPS)
            o_ref[rs, hs] = (y * _silu(g_ref[rs, hs])).astype(o_ref.dtype)
        state_ref[h] = state


def _retention(p32, batch, seq):
    lb = min(512, seq)
    nb = seq // lb
    t = batch * seq
    f32 = F32
    hh = N_HEADS
    c = RET_CHUNK
    log_gamma = jnp.log1p(-jnp.exp2(-5.0 - jnp.arange(hh, dtype=f32)))
    pos = jnp.arange(c, dtype=f32)
    rel = pos[:, None] - pos[None, :]
    causal = rel >= 0
    d_intra = jnp.where(causal, jnp.exp(jnp.where(causal, rel, 0.0) * log_gamma[:, None, None]), 0.0)
    d_q = jnp.exp((pos + 1.0) * log_gamma[:, None])
    d_k = jnp.exp((c - 1.0 - pos) * log_gamma[:, None])
    d_chunk = jnp.exp(c * log_gamma)
    dq_b = jnp.broadcast_to(d_q[:, :, None], (hh, c, HEAD_DIM))
    dk_b = jnp.broadcast_to(d_k[:, :, None], (hh, c, HEAD_DIM))
    inv = 1.0 / (10000.0 ** (jnp.arange(0, HEAD_DIM, 2, dtype=f32) / HEAD_DIM))
    ang = jnp.arange(seq).astype(f32)[:, None] * inv[None, :]
    cos_t = jnp.concatenate([jnp.cos(ang), jnp.cos(ang)], -1)
    sin_t = jnp.concatenate([-jnp.sin(ang), jnp.sin(ang)], -1)

    qkvg = lambda blk: pl.BlockSpec((lb, BRANCH_W), lambda b, j, blk=blk: (b * nb + j, blk))
    tab = pl.BlockSpec((lb, HEAD_DIM), lambda b, j: (j, 0))
    whole3 = pl.BlockSpec((hh, c, HEAD_DIM), lambda b, j: (0, 0, 0))
    return pl.pallas_call(
        functools.partial(_ret_kernel, n_chunks=lb // c),
        grid=(batch, nb),
        in_specs=[pl.BlockSpec(memory_space=pltpu.SMEM),
                  qkvg(3), qkvg(4), qkvg(5), qkvg(6), tab, tab, whole3, whole3, whole3],
        out_specs=pl.BlockSpec((lb, BRANCH_W), lambda b, j: (b * nb + j, 0)),
        out_shape=jax.ShapeDtypeStruct((t, BRANCH_W), BF16),
        scratch_shapes=[pltpu.VMEM((hh, HEAD_DIM, HEAD_DIM), F32)],
        compiler_params=_cparams(("parallel", "arbitrary")),
        name="retention",
    )(d_chunk, p32, p32, p32, p32, cos_t, sin_t, d_intra, dq_b, dk_b)


def _uq_kernel(x_ref, g_ref, w_ref, o_ref):
    x = x_ref[:, :DSA_Q_RANK]
    ms = jnp.mean(x * x, axis=-1, keepdims=True)
    xn = x * lax.rsqrt(ms + NORM_EPS) * g_ref[...]
    o_ref[...] = _dot(_bf(xn), w_ref[...]).astype(o_ref.dtype)


def _dsa_query(p32, q_norm, w_uq16):
    t = p32.shape[0]
    tm = min(512, t)
    return pl.pallas_call(
        _uq_kernel,
        grid=(t // tm,),
        in_specs=[pl.BlockSpec((tm, 512), lambda i: (i, 8)),
                  pl.BlockSpec((1, DSA_Q_RANK), lambda i: (0, 0)),
                  pl.BlockSpec((DSA_Q_RANK, UQ_W), lambda i: (0, 0))],
        out_specs=pl.BlockSpec((tm, UQ_W), lambda i: (i, 0)),
        out_shape=jax.ShapeDtypeStruct((t, UQ_W), BF16),
        compiler_params=_cparams(("parallel",)),
        name="dsa_query",
    )(p32, q_norm, w_uq16)


def _dsa_kernel(q_ref, qi_ref, small_ref, kidx_ref, k_ref, v_ref, o_ref, keys_ref,
                *, tq, kc, topk):
    i = pl.program_id(1)
    nkv = ((i + 1) * tq + kc - 1) // kc
    qpos = i * tq + lax.broadcasted_iota(I32, (1, tq), 1)
    kofs = lax.broadcasted_iota(I32, (kc, 1), 0)
    w_t = jnp.transpose(small_ref[...] * (IDX_HEADS ** -0.5) * (IDX_DIM ** -0.5))
    wrows = [w_t[SMALL_IDXW + h:SMALL_IDXW + h + 1, :] for h in range(IDX_HEADS)]
    qis = [qi_ref[:, h * LANES:(h + 1) * LANES] for h in range(IDX_HEADS)]

    def score_chunk(j, _):
        ks = pl.multiple_of(j * kc, kc)
        kix = kidx_ref[pl.ds(ks, kc), :]
        acc = jnp.zeros((kc, tq), F32)
        for h in range(IDX_HEADS):
            acc = acc + jnp.maximum(_dot_nt(kix, qis[h]), 0.0) * wrows[h]
        bits = pltpu.bitcast(acc, I32)
        key = jnp.where(bits < 0, bits ^ 0x7FFFFFFF, bits)
        causal = (j * kc + kofs) <= qpos
        keys_ref[j] = jnp.where(causal, key, INT_MIN)
        return 0

    lax.fori_loop(0, nkv, score_chunk, 0)

    def count_ge(cand):
        def cbody(j, part):
            hit = jnp.where(keys_ref[j] >= cand, 1.0, 0.0).reshape(kc // SUBLANES, SUBLANES, tq)
            return part + jnp.sum(hit, axis=0)
        part = lax.fori_loop(0, nkv, cbody, jnp.zeros((SUBLANES, tq), F32))
        return jnp.sum(part, axis=0, keepdims=True)

    def bit_step(it, theta):
        cand = theta + lax.shift_left(jnp.int32(1), 31 - it)
        return jnp.where(count_ge(cand) >= topk, cand, theta)

    theta = lax.fori_loop(0, 32, bit_step, jnp.full((1, tq), INT_MIN, I32))
    n_gt = jnp.where(theta == 2147483647, 0.0, count_ge(theta + 1))
    need = jnp.where(theta == INT_MIN, 0.0, topk - n_gt)

    ur = lax.broadcasted_iota(I32, (kc, kc), 0)
    uc = lax.broadcasted_iota(I32, (kc, kc), 1)
    upto = jnp.where(ur >= uc, 1.0, 0.0).astype(BF16)
    qs = [q_ref[:, h * HEAD_DIM:(h + 1) * HEAD_DIM] for h in range(N_HEADS)]
    scale = HEAD_DIM ** -0.5
    ninf = -jnp.inf

    def attend(j, carry):
        seen, ms, ls, accs = carry
        ks = pl.multiple_of(j * kc, kc)
        key = keys_ref[j]
        eq = key == theta
        pc = _dot(upto, jnp.where(eq, 1.0, 0.0).astype(BF16)) + seen
        rank = jnp.where(eq, pc - need, jnp.where(key > theta, -1.0, 1.0))
        sel = rank <= 0.0
        seen = pc[kc - 1:kc, :]
        kblk = k_ref[pl.ds(ks, kc), :]
        v_t = _bf(jnp.transpose(v_ref[pl.ds(ks, kc), :].astype(F32)))
        ms2, ls2, accs2 = [], [], []
        for h in range(N_HEADS):
            logit = jnp.where(sel, _dot_nt(kblk, qs[h]) * scale, ninf)
            m_new = jnp.maximum(ms[h], jnp.max(logit, axis=0, keepdims=True))
            p = jnp.exp(logit - m_new)
            a = jnp.exp(ms[h] - m_new)
            ls2.append(a * ls[h] + jnp.sum(p, axis=0, keepdims=True))
            accs2.append(a * accs[h] + _dot(v_t, _bf(p)))
            ms2.append(m_new)
        return seen, tuple(ms2), tuple(ls2), tuple(accs2)

    init = (jnp.zeros((1, tq), F32),
            tuple(jnp.full((1, tq), -1e30, F32) for _ in range(N_HEADS)),
            tuple(jnp.zeros((1, tq), F32) for _ in range(N_HEADS)),
            tuple(jnp.zeros((HEAD_DIM, tq), F32) for _ in range(N_HEADS)))
    _, _, ls, accs = lax.fori_loop(0, nkv, attend, init)
    for h in range(N_HEADS):
        o_ref[:, h * HEAD_DIM:(h + 1) * HEAD_DIM] = jnp.transpose(accs[h] / ls[h]).astype(o_ref.dtype)


def _dsa_attention(q_all, p32, p16, batch, seq):
    tq = min(256, seq)
    kc = min(256, seq)
    nq = seq // tq
    t = batch * seq
    topk = min(DSA_TOPK, seq // 4)
    kv = lambda blk: pl.BlockSpec((seq, LANES), lambda b, i, blk=blk: (b, blk))
    return pl.pallas_call(
        functools.partial(_dsa_kernel, tq=tq, kc=kc, topk=topk),
        grid=(batch, nq),
        in_specs=[pl.BlockSpec((tq, BRANCH_W), lambda b, i: (b * nq + i, UQ_Q_BLK)),
                  pl.BlockSpec((tq, IDX_HEADS * LANES), lambda b, i: (b * nq + i, 0)),
                  pl.BlockSpec((tq, LANES), lambda b, i: (b * nq + i, P32_SMALL_BLK)),
                  kv(P16_IDXK_BLK), kv(P16_DSA_K_BLK), kv(P16_DSA_V_BLK)],
        out_specs=pl.BlockSpec((tq, BRANCH_W), lambda b, i: (b * nq + i, 0)),
        out_shape=jax.ShapeDtypeStruct((t, BRANCH_W), BF16),
        scratch_shapes=[pltpu.VMEM((seq // kc, kc, tq), I32)],
        compiler_params=_cparams(("parallel", "arbitrary")),
        name="dsa_attention",
    )(q_all, q_all, p32, p16, p16, p16)


def _conv_kernel(prev_ref, x_ref, w_ref, o_ref, *, tl):
    first = pl.program_id(1) == 0
    prev = jnp.where(first, 0.0, prev_ref[...])
    ext = jnp.concatenate([prev, x_ref[...]], axis=0)
    acc = ext[SUBLANES:, :] * w_ref[GDN_CONV - 1:GDN_CONV, :]
    for d in range(1, GDN_CONV):
        acc = acc + pltpu.roll(ext, d, 0)[SUBLANES:, :] * w_ref[GDN_CONV - 1 - d:GDN_CONV - d, :]
    y = _silu(acc)
    is_v = pl.program_id(2) == 2
    for h in range(N_HEADS):
        hs = slice(h * HEAD_DIM, (h + 1) * HEAD_DIM)
        yh = y[:, hs]
        nrm = lax.rsqrt(jnp.sum(yh * yh, axis=-1, keepdims=True) + 1e-6)
        o_ref[:, hs] = yh * jnp.where(is_v, 1.0, nrm)


def _gdn_conv(p32, conv_w, batch, seq):
    tl = min(512, seq)
    nl = seq // tl
    t = batch * seq
    per8 = tl // SUBLANES
    return pl.pallas_call(
        functools.partial(_conv_kernel, tl=tl),
        grid=(batch, nl, 3),
        in_specs=[pl.BlockSpec((SUBLANES, BRANCH_W),
                               lambda b, i, c: (jnp.maximum((b * nl + i) * per8 - 1, 0), c)),
                  pl.BlockSpec((tl, BRANCH_W), lambda b, i, c: (b * nl + i, c)),
                  pl.BlockSpec((GDN_CONV, BRANCH_W), lambda b, i, c: (0, c))],
        out_specs=pl.BlockSpec((tl, BRANCH_W), lambda b, i, c: (b * nl + i, c)),
        out_shape=jax.ShapeDtypeStruct((t, 3 * BRANCH_W), F32),
        compiler_params=_cparams(("parallel", "parallel", "arbitrary")),
        name="gdn_conv",
    )(p32, p32, conv_w)


def _gdn_kernel(q_ref, k_ref, v_ref, gate_ref, small_ref, prm_ref, nrm_ref, o_ref, state_ref,
                *, n_chunks):
    c = GDN_CHUNK

    @pl.when(pl.program_id(1) == 0)
    def _():
        state_ref[...] = jnp.zeros_like(state_ref)

    lb = n_chunks * c
    g4 = N_HEADS * c
    ri = lax.broadcasted_iota(I32, (g4, g4), 0)
    cj = lax.broadcasted_iota(I32, (g4, g4), 1)
    sh = c.bit_length() - 1
    same = (ri >> sh) == (cj >> sh)
    incl = same & (ri >= cj)
    strict = same & (ri > cj)
    eye = jnp.where(ri == cj, 1.0, 0.0)
    li = lax.broadcasted_iota(I32, (lb, lb), 0)
    lj = lax.broadcasted_iota(I32, (lb, lb), 1)
    tri_chunks = jnp.where(((li >> sh) == (lj >> sh)) & (li >= lj), 1.0, 0.0).astype(BF16)

    small = small_ref[...]
    beta_all = _sigmoid(small)
    g_all = -jnp.exp(prm_ref[0:1, :]) * _softplus(small + prm_ref[1:2, :])
    gc_all = _dot_left_exact(tri_chunks, g_all)

    def stack(fn):
        return jnp.concatenate([fn(h) for h in range(N_HEADS)], axis=0)

    hsl = lambda h: slice(h * HEAD_DIM, (h + 1) * HEAD_DIM)
    scale = HEAD_DIM ** -0.5
    prepped = []
    for ci in range(n_chunks):
        rs = slice(ci * c, (ci + 1) * c)
        last = slice((ci + 1) * c - 1, (ci + 1) * c)
        k = stack(lambda h: k_ref[rs, hsl(h)])
        q = stack(lambda h: q_ref[rs, hsl(h)]) * scale
        v = stack(lambda h: v_ref[rs, hsl(h)])
        beta = stack(lambda h: beta_all[rs, SMALL_B + h:SMALL_B + h + 1])
        gc = stack(lambda h: gc_all[rs, SMALL_A + h:SMALL_A + h + 1])
        gc_last = stack(lambda h: jnp.broadcast_to(gc_all[last, SMALL_A + h:SMALL_A + h + 1], (c, 1)))
        gc_row = jnp.transpose(jnp.broadcast_to(gc, (g4, LANES)))[0:1, :]
        decay = jnp.where(incl, jnp.exp(jnp.where(incl, gc - gc_row, 0.0)), 0.0)
        k16 = _bf(k)
        k_beta = k * beta
        a = jnp.where(strict, _dot_nt(_bf(k_beta), k16) * decay, 0.0)
        nk = -a
        inv = eye + nk
        for _i in range(5):
            nk16 = _bf(nk)
            nk = _dot(nk16, nk16)
            inv = inv + _dot(_bf(inv), _bf(nk))
        e_gc = jnp.exp(gc)
        sol = _dot_hp(inv, jnp.concatenate([v * beta, k_beta * e_gc], axis=-1))
        qk = jnp.where(incl, _dot_nt(_bf(q), k16) * decay, 0.0)
        prepped.append(dict(u=sol[:, :HEAD_DIM], w16=_bf(sol[:, HEAD_DIM:]), qk16=_bf(qk),
                            qd16=_bf(q * e_gc), kd16=_bf(k * jnp.exp(gc_last - gc)),
                            cdec=[jnp.exp(gc_all[last, SMALL_A + h:SMALL_A + h + 1]) for h in range(N_HEADS)]))

    states = [state_ref[h] for h in range(N_HEADS)]
    for ci, p in enumerate(prepped):
        rs = slice(ci * c, (ci + 1) * c)
        row = lambda h: slice(h * c, (h + 1) * c)
        s16 = [_bf(s) for s in states]
        vn16 = _bf(stack(lambda h: p['u'][row(h)] - _dot(p['w16'][row(h)], s16[h])))
        o = _dot(p['qk16'], vn16) + stack(lambda h: _dot(p['qd16'][row(h)], s16[h]))
        states = [states[h] * p['cdec'][h] + _dot_tn(p['kd16'][row(h)], vn16[row(h)])
                  for h in range(N_HEADS)]
        for h in range(N_HEADS):
            oh = o[row(h)]
            ms = jnp.mean(oh * oh, axis=-1, keepdims=True)
            y = oh * lax.rsqrt(ms + NORM_EPS) * nrm_ref[...]
            o_ref[rs, hsl(h)] = (y * _silu(gate_ref[rs, hsl(h)])).astype(o_ref.dtype)
    for h in range(N_HEADS):
        state_ref[h] = states[h]


def _gated_deltanet(gq, p32, prm, gnorm, batch, seq):
    lb = min(256, seq)
    nb = seq // lb
    t = batch * seq
    blk = lambda cb: pl.BlockSpec((lb, BRANCH_W), lambda b, j, cb=cb: (b * nb + j, cb))
    return pl.pallas_call(
        functools.partial(_gdn_kernel, n_chunks=lb // GDN_CHUNK),
        grid=(batch, nb),
        in_specs=[blk(0), blk(1), blk(2), blk(7),
                  pl.BlockSpec((lb, LANES), lambda b, j: (b * nb + j, P32_SMALL_BLK)),
                  pl.BlockSpec((SUBLANES, LANES), lambda b, j: (0, 0)),
                  pl.BlockSpec((1, HEAD_DIM), lambda b, j: (0, 0))],
        out_specs=pl.BlockSpec((lb, BRANCH_W), lambda b, j: (b * nb + j, 0)),
        out_shape=jax.ShapeDtypeStruct((t, BRANCH_W), BF16),
        scratch_shapes=[pltpu.VMEM((N_HEADS, HEAD_DIM, HEAD_DIM), F32)],
        compiler_params=_cparams(("parallel", "arbitrary")),
        name="gated_deltanet",
    )(gq, gq, gq, p32, p32, prm, gnorm)


def _merge_kernel(x_ref, ya_ref, yb_ref, yc_ref, yd_ref, wg_ref, wb_ref, o_ref):
    x = x_ref[...]
    acc = None
    for i, y_ref in enumerate((ya_ref, yb_ref, yc_ref, yd_ref)):
        term = _sigmoid(_dot(x, wg_ref[i])) * _dot(y_ref[...], wb_ref[i])
        acc = term if acc is None else acc + term
    o_ref[...] = acc.astype(o_ref.dtype)


def _merge(x16, ys, wgl, wbl):
    (wg16, l), (wb16, _) = wgl, wbl
    t = x16.shape[0]
    tm = min(1024, t)
    tn = 256
    yspec = pl.BlockSpec((tm, BRANCH_W), lambda i, j: (i, 0))
    return pl.pallas_call(
        _merge_kernel,
        grid=(t // tm, D_MODEL // tn),
        in_specs=[pl.BlockSpec((tm, D_MODEL), lambda i, j: (i, 0)), yspec, yspec, yspec, yspec,
                  pl.BlockSpec((None, 4, D_MODEL, tn), lambda i, j: (l, 0, 0, j)),
                  pl.BlockSpec((None, 4, BRANCH_W, tn), lambda i, j: (l, 0, 0, j))],
        out_specs=pl.BlockSpec((tm, tn), lambda i, j: (i, j)),
        out_shape=jax.ShapeDtypeStruct((t, D_MODEL), BF16),
        compiler_params=_cparams(("parallel", "arbitrary")),
        name="gated_merge",
    )(x16, *ys, wg16, wb16)


def _layer_norm_rows(h, g, b):
    mu = jnp.mean(h, axis=-1, keepdims=True)
    d = h - mu
    var = jnp.mean(d * d, axis=-1, keepdims=True)
    return d * lax.rsqrt(var + NORM_EPS) * g + b


def _outproj_ln_kernel(m_ref, w_ref, x_ref, g_ref, b_ref, o32_ref, o16_ref):
    h = ALPHA * x_ref[...] + _dot(m_ref[...], w_ref[...])
    y = _layer_norm_rows(h, g_ref[...], b_ref[...])
    o32_ref[...] = y
    o16_ref[...] = _bf(y)


def _outproj_ln(merged16, woutl, x32, g, b):
    w_out16, l = woutl
    t = x32.shape[0]
    tm = min(512, t)
    row = pl.BlockSpec((tm, D_MODEL), lambda i: (i, 0))
    vec = pl.BlockSpec((1, D_MODEL), lambda i: (0, 0))
    return pl.pallas_call(
        _outproj_ln_kernel,
        grid=(t // tm,),
        in_specs=[row, pl.BlockSpec((None, D_MODEL, D_MODEL), lambda i: (l, 0, 0)), row, vec, vec],
        out_specs=[row, row],
        out_shape=[jax.ShapeDtypeStruct((t, D_MODEL), F32), jax.ShapeDtypeStruct((t, D_MODEL), BF16)],
        compiler_params=_cparams(("parallel",)),
        name="outproj_ln",
    )(merged16, w_out16, x32, g, b)


def _router_kernel(x_ref, whi_ref, wlo_ref, b_ref, ids_ref, gates_ref, sizes_ref, run_ref):
    @pl.when(pl.program_id(0) == 0)
    def _():
        run_ref[...] = jnp.zeros_like(run_ref)

    xh, xl = _split2(x_ref[...])
    whi = whi_ref[...]
    logits = _dot(xh, whi) + _dot(xh, wlo_ref[...]) + _dot(xl, whi) + b_ref[...]
    tm = logits.shape[0]
    lane = lax.broadcasted_iota(I32, (tm, LANES), 1)
    ninf = -jnp.inf
    gl = jnp.where(lane < N_GROUPS, logits, ninf)
    gmax = jnp.max(gl, axis=-1, keepdims=True)
    grp = jnp.min(jnp.where(gl == gmax, lane, LANES), axis=-1, keepdims=True)
    p_grp = 1.0 / jnp.sum(jnp.where(lane < N_GROUPS, jnp.exp(logits - gmax), 0.0), axis=-1, keepdims=True)
    lo = N_GROUPS + grp * EXPERTS_PER_GROUP
    el = jnp.where((lane >= lo) & (lane < lo + EXPERTS_PER_GROUP), logits, ninf)
    v1 = jnp.max(el, axis=-1, keepdims=True)
    i1 = jnp.min(jnp.where(el == v1, lane, LANES), axis=-1, keepdims=True)
    el2 = jnp.where(lane == i1, ninf, el)
    v2 = jnp.max(el2, axis=-1, keepdims=True)
    i2 = jnp.min(jnp.where(el2 == v2, lane, LANES), axis=-1, keepdims=True)
    e21 = jnp.exp(v2 - v1)
    den = 1.0 / (1.0 + e21)
    g1 = p_grp * den
    g2 = p_grp * (e21 * den)
    oh = jnp.where(lane == i1, 1.0, jnp.where(lane == i2, 1.0, 0.0))
    tr = lax.broadcasted_iota(I32, (tm, tm), 0)
    tc = lax.broadcasted_iota(I32, (tm, tm), 1)
    earlier = jnp.where(tr > tc, 1.0, 0.0).astype(BF16)
    before = _dot(earlier, _bf(oh)) + run_ref[0:1, :]
    r1 = jnp.sum(jnp.where(lane == i1, before, 0.0), axis=-1, keepdims=True).astype(I32)
    r2 = jnp.sum(jnp.where(lane == i2, before, 0.0), axis=-1, keepdims=True).astype(I32)
    run_new = before[tm - 1:tm, :] + oh[tm - 1:tm, :]
    run_ref[0:1, :] = run_new
    sizes_ref[...] = jnp.broadcast_to(run_new, sizes_ref.shape)
    ids_ref[...] = jnp.where(lane == 0, i1 - N_GROUPS, jnp.where(lane == 1, i2 - N_GROUPS,
                             jnp.where(lane == 2, r1, jnp.where(lane == 3, r2, 0))))
    gates_ref[...] = jnp.where(lane == 0, g1, jnp.where(lane == 1, g2, 0.0))


def _router(x32, whi, wlo, bias):
    t = x32.shape[0]
    tm = min(512, t)
    row = pl.BlockSpec((tm, LANES), lambda i: (i, 0))
    wsp = pl.BlockSpec((D_MODEL, LANES), lambda i: (0, 0))
    return pl.pallas_call(
        _router_kernel,
        grid=(t // tm,),
        in_specs=[pl.BlockSpec((tm, D_MODEL), lambda i: (i, 0)), wsp, wsp,
                  pl.BlockSpec((1, LANES), lambda i: (0, 0))],
        out_specs=[row, row, pl.BlockSpec((SUBLANES, LANES), lambda i: (0, 0))],
        out_shape=[jax.ShapeDtypeStruct((t, LANES), I32), jax.ShapeDtypeStruct((t, LANES), F32),
                   jax.ShapeDtypeStruct((SUBLANES, LANES), F32)],
        scratch_shapes=[pltpu.VMEM((SUBLANES, LANES), F32)],
        compiler_params=_cparams(("arbitrary",)),
        name="moe_router",
    )(x32, whi, wlo, bias)


DMA_UNROLL = 8


GATHER_DEPTH = 3


def _expert_kernel(be_ref, nb_ref, tok_ref, tok_next_ref, tok_next2_ref, x_hbm, wg_ref, wu_ref, wd_ref,
                   o_ref, xbuf, wg16, wu16, wd16, sem, *, blk):
    i = pl.program_id(0)
    nb = nb_ref[0]
    par = lax.rem(i, GATHER_DEPTH)

    def row_copy(r, tok, p):
        return pltpu.make_async_copy(x_hbm.at[pl.ds(tok, 1), :], xbuf.at[p, pl.ds(r, 1), :], sem.at[p])

    def issue(tref, p):
        def body(g, _):
            for u in range(DMA_UNROLL):
                r = g * DMA_UNROLL + u
                row_copy(r, tref[0, 0, r], p).start(priority=u % 2)
            return 0
        lax.fori_loop(0, blk // DMA_UNROLL, body, 0)

    @pl.when((i == 0) | (be_ref[i] != be_ref[jnp.maximum(i - 1, 0)]))
    def _():
        wg16[...] = _bf(wg_ref[...])
        wu16[...] = _bf(wu_ref[...])
        wd16[...] = _bf(wd_ref[...])

    @pl.when(i == 0)
    def _():
        issue(tok_ref, 0)

        @pl.when(1 < nb)
        def _():
            issue(tok_next_ref, 1)

    @pl.when(i + 2 < nb)
    def _():
        issue(tok_next2_ref, lax.rem(i + 2, GATHER_DEPTH))

    @pl.when(i < nb)
    def _():
        def wait(g, _):
            for u in range(DMA_UNROLL):
                row_copy(0, 0, par).wait()
            return 0
        lax.fori_loop(0, blk // DMA_UNROLL, wait, 0)
        xe = _bf(xbuf[par])
        h = _silu(_dot(xe, wg16[...])) * _dot(xe, wu16[...])
        o_ref[...] = _dot(_bf(h), wd16[...])

    @pl.when(i >= nb)
    def _():
        o_ref[...] = jnp.zeros_like(o_ref)


def _experts(x32, slot_tok, block_e, nb_used, wgl, wul, wdl, blk):
    (wg16, l), (wu16, _), (wd16, _) = wgl, wul, wdl
    n_blocks = block_e.shape[0]
    wspec = lambda shape: pl.BlockSpec((None, None) + shape, lambda i, be, nb: (l, be[i], 0, 0))
    tok3 = slot_tok.reshape(n_blocks, 1, blk)
    grid_spec = pltpu.PrefetchScalarGridSpec(
        num_scalar_prefetch=2,
        grid=(n_blocks,),
        in_specs=[pl.BlockSpec((1, 1, blk), lambda i, be, nb: (i, 0, 0), memory_space=pltpu.SMEM),
                  pl.BlockSpec((1, 1, blk), lambda i, be, nb: (jnp.minimum(i + 1, n_blocks - 1), 0, 0),
                               memory_space=pltpu.SMEM),
                  pl.BlockSpec((1, 1, blk), lambda i, be, nb: (jnp.minimum(i + 2, n_blocks - 1), 0, 0),
                               memory_space=pltpu.SMEM),
                  pl.BlockSpec(memory_space=pl.ANY),
                  wspec((D_MODEL, D_EXPERT)), wspec((D_MODEL, D_EXPERT)), wspec((D_EXPERT, D_MODEL))],
        out_specs=pl.BlockSpec((blk, D_MODEL), lambda i, be, nb: (i, 0)),
        scratch_shapes=[pltpu.VMEM((GATHER_DEPTH, blk, D_MODEL), F32),
                        pltpu.VMEM((D_MODEL, D_EXPERT), BF16), pltpu.VMEM((D_MODEL, D_EXPERT), BF16),
                        pltpu.VMEM((D_EXPERT, D_MODEL), BF16), pltpu.SemaphoreType.DMA((GATHER_DEPTH,))],
    )
    return pl.pallas_call(
        functools.partial(_expert_kernel, blk=blk),
        grid_spec=grid_spec,
        out_shape=jax.ShapeDtypeStruct((n_blocks * blk, D_MODEL), F32),
        compiler_params=_cparams(("arbitrary",)),
        name="moe_experts",
    )(block_e, nb_used, tok3, tok3, tok3, x32, wg16, wu16, wd16)


def _combine_kernel(slot_ref, slot_next_ref, y_hbm, x_ref, gates_ref, g_ref, b_ref, o32_ref, o16_ref,
                    ybuf, sem, *, tm):
    i = pl.program_id(0)
    n = pl.num_programs(0)
    par = lax.rem(i, 2)

    def row_copy(r, k, slot, p):
        return pltpu.make_async_copy(y_hbm.at[pl.ds(slot, 1), :], ybuf.at[p, k, pl.ds(r, 1), :], sem.at[p])

    def issue(sref, p):
        def body(g, _):
            for u in range(DMA_UNROLL):
                r = g * DMA_UNROLL + u
                row_copy(r, 0, sref[0, 0, 2 * r], p).start(priority=0)
                row_copy(r, 1, sref[0, 0, 2 * r + 1], p).start(priority=1)
            return 0
        lax.fori_loop(0, tm // DMA_UNROLL, body, 0)

    @pl.when(i == 0)
    def _():
        issue(slot_ref, 0)

    @pl.when(i + 1 < n)
    def _():
        issue(slot_next_ref, 1 - par)

    def wait(g, _):
        for u in range(2 * DMA_UNROLL):
            row_copy(0, 0, 0, par).wait()
        return 0

    lax.fori_loop(0, tm // DMA_UNROLL, wait, 0)
    gates = gates_ref[...]
    y = ybuf[par, 0] * gates[:, 0:1] + ybuf[par, 1] * gates[:, 1:2]
    out = _layer_norm_rows(ALPHA * x_ref[...] + y, g_ref[...], b_ref[...])
    o32_ref[...] = out
    o16_ref[...] = _bf(out)


def _combine_ln(slots, yb, x32, gates, g, b):
    t = x32.shape[0]
    tm = min(256, t)
    n = t // tm
    row = pl.BlockSpec((tm, D_MODEL), lambda i: (i, 0))
    vec = pl.BlockSpec((1, D_MODEL), lambda i: (0, 0))
    slots3 = slots.reshape(n, 1, 2 * tm)
    return pl.pallas_call(
        functools.partial(_combine_kernel, tm=tm),
        grid=(n,),
        in_specs=[pl.BlockSpec((1, 1, 2 * tm), lambda i: (i, 0, 0), memory_space=pltpu.SMEM),
                  pl.BlockSpec((1, 1, 2 * tm), lambda i: (jnp.minimum(i + 1, n - 1), 0, 0),
                               memory_space=pltpu.SMEM),
                  pl.BlockSpec(memory_space=pl.ANY), row,
                  pl.BlockSpec((tm, LANES), lambda i: (i, 0)), vec, vec],
        out_specs=[row, row],
        out_shape=[jax.ShapeDtypeStruct((t, D_MODEL), F32), jax.ShapeDtypeStruct((t, D_MODEL), BF16)],
        scratch_shapes=[pltpu.VMEM((2, 2, tm, D_MODEL), F32), pltpu.SemaphoreType.DMA((2,))],
        compiler_params=_cparams(("arbitrary",)),
        name="moe_combine_ln",
    )(slots3, slots3, yb, x32, gates, g, b)


def _moe_plan(ids, sizes_row, blk):
    t = ids.shape[0]
    sizes = sizes_row[0, N_GROUPS:N_GROUPS + N_EXPERTS].astype(I32)
    padded = (sizes + blk - 1) // blk * blk
    pad_ends = jnp.cumsum(padded)
    pad_starts = pad_ends - padded
    slots = (pad_starts[ids[:, 0:2]] + ids[:, 2:4]).astype(I32)
    n_blocks = 2 * t // blk + N_EXPERTS
    tok_of = jnp.broadcast_to(jnp.arange(t, dtype=I32)[:, None], (t, 2))
    slot_tok = jnp.zeros((n_blocks * blk,), I32).at[slots.reshape(-1)].set(tok_of.reshape(-1))
    starts = jnp.arange(n_blocks, dtype=I32) * blk
    block_e = jnp.minimum(jnp.sum((pad_ends[None, :] <= starts[:, None]).astype(I32), axis=-1),
                          N_EXPERTS - 1).astype(I32)
    nb_used = (pad_ends[-1:] // blk).astype(I32)
    return slots, slot_tok, block_e, nb_used


MOE_BLK = 256
STACKED = ('w16', 'w32', 'wg', 'wb', 'wout', 'eg', 'eu', 'ed')
IN_TM = 1024
IN_TN16, IN_TN32 = 1024, 768


def _prep_weights(w_in, dsa_q_norm, w_uq, gdn_conv, gdn_a_log, gdn_dt_bias, gdn_norm, w_branch,
                  w_branch_gate, w_out, ln1_g, ln1_b, w_router_group, b_router_group,
                  w_router_expert, b_router_expert, w_exp_gate, w_exp_up, w_exp_down, ln2_g, ln2_b):
    depth = w_in.shape[0]
    zc = lambda n: jnp.zeros((depth, D_MODEL, n), F32)
    c = lambda a, b: w_in[..., a:b]
    w16 = jnp.concatenate([c(0, 1536), c(3968, 4224), c(4224, 4288), zc(P16_W - 1856)], -1).astype(BF16)
    w32 = jnp.concatenate([c(4296, 5832), c(1536, 3584), c(5840, 6352), c(3584, 3968),
                           c(4288, 4296), c(5832, 5836), c(5836, 5840), zc(LANES - 16)], -1).astype(BF16)
    qi = w_uq[..., BRANCH_W:].reshape(depth, DSA_Q_RANK, IDX_HEADS, IDX_DIM)
    qi = jnp.pad(qi, ((0, 0), (0, 0), (0, 0), (0, LANES - IDX_DIM))).reshape(depth, DSA_Q_RANK, IDX_HEADS * LANES)
    wuq = jnp.concatenate([qi, w_uq[..., :BRANCH_W]], -1).astype(BF16)
    wr = jnp.concatenate([w_router_group, w_router_expert,
                          zc(LANES - N_GROUPS - N_EXPERTS)], -1)
    rhi = wr.astype(BF16)
    rlo = (wr - rhi.astype(F32)).astype(BF16)
    rb = jnp.concatenate([b_router_group, b_router_expert,
                          jnp.zeros((depth, LANES - N_GROUPS - N_EXPERTS), F32)], -1)[:, None, :]
    prm = jnp.zeros((depth, SUBLANES, LANES), F32)
    prm = prm.at[:, 0, SMALL_A:SMALL_A + N_HEADS].set(gdn_a_log)
    prm = prm.at[:, 1, SMALL_A:SMALL_A + N_HEADS].set(gdn_dt_bias)
    return dict(
        w16=w16, w32=w32, q_norm=dsa_q_norm[:, None, :], wuq=wuq, conv=gdn_conv, prm=prm,
        gnorm=gdn_norm[:, None, :], wb=w_branch.astype(BF16), wg=w_branch_gate.astype(BF16),
        wout=w_out.astype(BF16), ln1g=ln1_g[:, None, :], ln1b=ln1_b[:, None, :], rhi=rhi, rlo=rlo, rb=rb,
        eg=w_exp_gate, eu=w_exp_up, ed=w_exp_down,
        ln2g=ln2_g[:, None, :], ln2b=ln2_b[:, None, :])


def _mixer(x16, lw, batch, seq):
    p16 = _matmul(x16, lw['w16'], BF16, IN_TM, IN_TN16, "in_proj_bf16")
    p32 = _matmul(x16, lw['w32'], F32, IN_TM, IN_TN32, "in_proj_f32")
    y_a = _sb_attention(p16, batch, seq)
    y_b = _retention(p32, batch, seq)
    q_all = _dsa_query(p32, lw['q_norm'], lw['wuq'])
    y_c = _dsa_attention(q_all, p32, p16, batch, seq)
    gq = _gdn_conv(p32, lw['conv'], batch, seq)
    y_d = _gated_deltanet(gq, p32, lw['prm'], lw['gnorm'], batch, seq)
    return _merge(x16, (y_a, y_b, y_c, y_d), lw['wg'], lw['wb'])


def _moe(x32, lw):
    ids, gates, sizes = _router(x32, lw['rhi'], lw['rlo'], lw['rb'])
    slots, slot_tok, block_e, nb_used = _moe_plan(ids, sizes, MOE_BLK)
    yb = _experts(x32, slot_tok, block_e, nb_used, lw['eg'], lw['eu'], lw['ed'], MOE_BLK)
    return _combine_ln(slots, yb, x32, gates, lw['ln2g'], lw['ln2b'])


def _layer_weights(ws, l):
    return {k: ((v, l) if k in STACKED else v[l]) for k, v in ws.items()}


def _layer(x32, x16, lw, batch, seq):
    merged = _mixer(x16, lw, batch, seq)
    x32, x16 = _outproj_ln(merged, lw['wout'], x32, lw['ln1g'], lw['ln1b'])
    return _moe(x32, lw)


def kernel(x, w_in, dsa_q_norm, w_uq, gdn_conv, gdn_a_log, gdn_dt_bias, gdn_norm, w_branch, w_branch_gate,
           w_out, ln1_g, ln1_b, w_router_group, b_router_group, w_router_expert, b_router_expert,
           w_exp_gate, w_exp_up, w_exp_down, ln2_g, ln2_b):
    batch, seq, d = x.shape
    assert d == D_MODEL and seq % RET_CHUNK == 0
    ws = _prep_weights(w_in, dsa_q_norm, w_uq, gdn_conv, gdn_a_log, gdn_dt_bias, gdn_norm, w_branch,
                       w_branch_gate, w_out, ln1_g, ln1_b, w_router_group, b_router_group,
                       w_router_expert, b_router_expert, w_exp_gate, w_exp_up, w_exp_down, ln2_g, ln2_b)
    x32 = x.reshape(batch * seq, d)
    x16 = x32.astype(BF16)
    for l in range(w_in.shape[0]):
        x32, x16 = _layer(x32, x16, _layer_weights(ws, l), batch, seq)
    return x32.reshape(batch, seq, d)
```

```python
import functools
import math

import numpy as np
import jax
import jax.numpy as jnp
from jax import lax
from jax.experimental import pallas as pl
from jax.experimental.pallas import tpu as pltpu

F32 = jnp.float32
BF16 = jnp.bfloat16
I32 = jnp.int32

D_MODEL = 2048
DEPTH = 4
HEAD_DIM = 128
N_HEADS = 4
BRANCH_W = N_HEADS * HEAD_DIM
RET_CHUNK = 128
DSA_Q_RANK = 384
DSA_TOPK = 256
IDX_HEADS = 8
IDX_DIM = 64
GDN_CONV = 4
GDN_CHUNK = 64
N_GROUPS = 4
EXPERTS_PER_GROUP = 8
N_EXPERTS = N_GROUPS * EXPERTS_PER_GROUP
D_EXPERT = 512
ALPHA = (2.0 * DEPTH) ** 0.25
NORM_EPS = 1e-5

LANES = 128
SUBLANES = 8
VMEM_LIMIT_BYTES = 56 * 1024 * 1024

P16_W = 2048
P16_DSA_K_BLK = 12
P16_DSA_V_BLK = 13
P16_IDXK_BLK = 14
P32_W = 4608
P32_SMALL_BLK = 35
SMALL_IDXW, SMALL_B, SMALL_A = 0, 8, 12
UQ_W = IDX_HEADS * LANES + BRANCH_W
UQ_Q_BLK = IDX_HEADS * LANES // BRANCH_W

INT_MIN = -2147483648


def _cparams(sem):
    return pltpu.CompilerParams(dimension_semantics=sem, vmem_limit_bytes=VMEM_LIMIT_BYTES)


def _bf(x):
    return x.astype(BF16)


def _dot(a, b):
    return jnp.dot(a, b, preferred_element_type=F32)


def _dot_nt(a, b):
    return lax.dot_general(a, b, (((1,), (1,)), ((), ())), preferred_element_type=F32)


def _dot_tn(a, b):
    return lax.dot_general(a, b, (((0,), (0,)), ((), ())), preferred_element_type=F32)


def _split2(x):
    hi = _bf(x)
    lo = _bf(x - hi.astype(F32))
    return hi, lo


def _split3(x):
    hi = _bf(x)
    r = x - hi.astype(F32)
    mid = _bf(r)
    lo = _bf(r - mid.astype(F32))
    return hi, mid, lo


def _dot_left_exact(m01, x):
    hi, mid, lo = _split3(x)
    return _dot(m01, hi) + _dot(m01, mid) + _dot(m01, lo)


def _dot_right_exact(x, m01):
    hi, mid, lo = _split3(x)
    return _dot(hi, m01) + _dot(mid, m01) + _dot(lo, m01)


def _dot_hp(a, b):
    ah, al = _split2(a)
    bh, bl = _split2(b)
    return _dot(ah, bh) + _dot(ah, bl) + _dot(al, bh)


def _silu(x):
    return x * (1.0 / (1.0 + jnp.exp(-x)))


def _sigmoid(x):
    return 1.0 / (1.0 + jnp.exp(-x))


def _softplus(x):
    return jnp.maximum(x, 0.0) + jnp.log1p(jnp.exp(-jnp.abs(x)))


def _mm_kernel(x_ref, w_ref, o_ref):
    o_ref[...] = _dot(x_ref[...], w_ref[...]).astype(o_ref.dtype)


def _matmul(x, wl, out_dtype, tm, tn, name):
    w, l = wl
    m, k = x.shape
    n = w.shape[2]
    tm = min(tm, m)
    return pl.pallas_call(
        _mm_kernel,
        grid=(m // tm, n // tn),
        in_specs=[pl.BlockSpec((tm, k), lambda i, j: (i, 0)),
                  pl.BlockSpec((None, k, tn), lambda i, j: (l, 0, j))],
        out_specs=pl.BlockSpec((tm, tn), lambda i, j: (i, j)),
        out_shape=jax.ShapeDtypeStruct((m, n), out_dtype),
        compiler_params=_cparams(("parallel", "arbitrary")),
        name=name,
    )(x, w)


def _sb_kernel(q_ref, k_ref, v_ref, o_ref, *, tq):
    i = pl.program_id(1)
    scale = HEAD_DIM ** -0.5
    hsl = lambda h: slice(h * HEAD_DIM, (h + 1) * HEAD_DIM)
    qs = [q_ref[:, hsl(h)] for h in range(N_HEADS)]
    jr = lax.broadcasted_iota(I32, (tq, tq), 0)
    jc = lax.broadcasted_iota(I32, (tq, tq), 1)
    below = jr > jc
    later = jnp.where(below, 1.0, 0.0).astype(BF16)

    def block(kb, carry, diagonal):
        ks = pl.multiple_of(kb * tq, tq)
        out = []
        for h in range(N_HEADS):
            acc, run = carry[h]
            kblk = k_ref[pl.ds(ks, tq), hsl(h)]
            vblk = v_ref[pl.ds(ks, tq), hsl(h)]
            z = _dot_nt(qs[h], kblk) * scale
            sp = jnp.log(1.0 + jnp.exp(-jnp.abs(z)))
            log_beta = jnp.minimum(z, 0.0) - sp
            log_1m = log_beta - z
            if diagonal:
                log_1m = jnp.where(below, log_1m, 0.0)
            hi, lo = _split2(log_1m)
            suffix = _dot(hi, later) + _dot(lo, later)
            w = jnp.exp(log_beta + suffix + run)
            if diagonal:
                w = jnp.where(below, w, 0.0)
            out.append((acc + _dot(_bf(w), vblk), run + suffix[:, 0:1] + log_1m[:, 0:1]))
        return tuple(out)

    init = tuple((jnp.zeros((tq, HEAD_DIM), F32), jnp.zeros((tq, 1), F32)) for _ in range(N_HEADS))
    res = block(i, init, True)

    def pair(jj, carry):
        kb = i - 1 - 2 * jj
        return block(kb - 1, block(kb, carry, False), False)

    res = lax.fori_loop(0, i // 2, pair, res)
    res = lax.cond(lax.rem(i, 2) == 1, lambda c: block(0, c, False), lambda c: c, res)
    for h in range(N_HEADS):
        o_ref[:, hsl(h)] = res[h][0].astype(o_ref.dtype)


def _sb_attention(p16, batch, seq):
    tq = min(256, seq)
    nq = seq // tq
    t = batch * seq
    return pl.pallas_call(
        functools.partial(_sb_kernel, tq=tq),
        grid=(batch, nq),
        in_specs=[pl.BlockSpec((tq, BRANCH_W), lambda b, i: (b * nq + i, 0)),
                  pl.BlockSpec((seq, BRANCH_W), lambda b, i: (b, 1)),
                  pl.BlockSpec((seq, BRANCH_W), lambda b, i: (b, 2))],
        out_specs=pl.BlockSpec((tq, BRANCH_W), lambda b, i: (b * nq + i, 0)),
        out_shape=jax.ShapeDtypeStruct((t, BRANCH_W), BF16),
        compiler_params=_cparams(("parallel", "arbitrary")),
        name="sb_attention",
    )(p16, p16, p16)


def _ret_kernel(dch_ref, q_ref, k_ref, v_ref, g_ref, cos_ref, sin_ref, dintra_ref, dq_ref, dk_ref,
                o_ref, state_ref, *, n_chunks):
    c = RET_CHUNK

    @pl.when(pl.program_id(1) == 0)
    def _():
        state_ref[...] = jnp.zeros_like(state_ref)

    def rot(x, cs, sn):
        return x * cs + pltpu.roll(x, HEAD_DIM // 2, 1) * sn

    for h in range(N_HEADS):
        hs = slice(h * HEAD_DIM, (h + 1) * HEAD_DIM)
        state = state_ref[h]
        for ci in range(n_chunks):
            rs = slice(ci * c, (ci + 1) * c)
            cs = cos_ref[rs, :]
            sn = sin_ref[rs, :]
            qc = rot(q_ref[rs, hs], cs, sn)
            kc = rot(k_ref[rs, hs], cs, sn) * (HEAD_DIM ** -0.5)
            vc = _bf(v_ref[rs, hs])
            qb = _bf(qc)
            s = _dot_nt(qb, _bf(kc)) * dintra_ref[h]
            o = _dot(_bf(s), vc) + _dot(qb, _bf(state)) * dq_ref[h]
            kv = _dot_tn(_bf(kc * dk_ref[h]), vc)
            state = dch_ref[h] * state + kv
            mu = jnp.mean(o, axis=-1, keepdims=True)
            d = o - mu
            var = jnp.mean(d * d, axis=-1, keepdims=True)
            y = d * lax.rsqrt(var + NORM_EPS)
            o_ref[rs, hs] = (y * _silu(g_ref[rs, hs])).astype(o_ref.dtype)
        state_ref[h] = state


def _retention(p32, batch, seq):
    lb = min(512, seq)
    nb = seq // lb
    t = batch * seq
    f32 = F32
    hh = N_HEADS
    c = RET_CHUNK
    log_gamma = jnp.log1p(-jnp.exp2(-5.0 - jnp.arange(hh, dtype=f32)))
    pos = jnp.arange(c, dtype=f32)
    rel = pos[:, None] - pos[None, :]
    causal = rel >= 0
    d_intra = jnp.where(causal, jnp.exp(jnp.where(causal, rel, 0.0) * log_gamma[:, None, None]), 0.0)
    d_q = jnp.exp((pos + 1.0) * log_gamma[:, None])
    d_k = jnp.exp((c - 1.0 - pos) * log_gamma[:, None])
    d_chunk = jnp.exp(c * log_gamma)
    dq_b = jnp.broadcast_to(d_q[:, :, None], (hh, c, HEAD_DIM))
    dk_b = jnp.broadcast_to(d_k[:, :, None], (hh, c, HEAD_DIM))
    inv = 1.0 / (10000.0 ** (jnp.arange(0, HEAD_DIM, 2, dtype=f32) / HEAD_DIM))
    ang = jnp.arange(seq).astype(f32)[:, None] * inv[None, :]
    cos_t = jnp.concatenate([jnp.cos(ang), jnp.cos(ang)], -1)
    sin_t = jnp.concatenate([-jnp.sin(ang), jnp.sin(ang)], -1)

    qkvg = lambda blk: pl.BlockSpec((lb, BRANCH_W), lambda b, j, blk=blk: (b * nb + j, blk))
    tab = pl.BlockSpec((lb, HEAD_DIM), lambda b, j: (j, 0))
    whole3 = pl.BlockSpec((hh, c, HEAD_DIM), lambda b, j: (0, 0, 0))
    return pl.pallas_call(
        functools.partial(_ret_kernel, n_chunks=lb // c),
        grid=(batch, nb),
        in_specs=[pl.BlockSpec(memory_space=pltpu.SMEM),
                  qkvg(3), qkvg(4), qkvg(5), qkvg(6), tab, tab, whole3, whole3, whole3],
        out_specs=pl.BlockSpec((lb, BRANCH_W), lambda b, j: (b * nb + j, 0)),
        out_shape=jax.ShapeDtypeStruct((t, BRANCH_W), BF16),
        scratch_shapes=[pltpu.VMEM((hh, HEAD_DIM, HEAD_DIM), F32)],
        compiler_params=_cparams(("parallel", "arbitrary")),
        name="retention",
    )(d_chunk, p32, p32, p32, p32, cos_t, sin_t, d_intra, dq_b, dk_b)


def _uq_kernel(x_ref, g_ref, w_ref, o_ref):
    x = x_ref[:, :DSA_Q_RANK]
    ms = jnp.mean(x * x, axis=-1, keepdims=True)
    xn = x * lax.rsqrt(ms + NORM_EPS) * g_ref[...]
    o_ref[...] = _dot(_bf(xn), w_ref[...]).astype(o_ref.dtype)


def _dsa_query(p32, q_norm, w_uq16):
    t = p32.shape[0]
    tm = min(512, t)
    return pl.pallas_call(
        _uq_kernel,
        grid=(t // tm,),
        in_specs=[pl.BlockSpec((tm, 512), lambda i: (i, 8)),
                  pl.BlockSpec((1, DSA_Q_RANK), lambda i: (0, 0)),
                  pl.BlockSpec((DSA_Q_RANK, UQ_W), lambda i: (0, 0))],
        out_specs=pl.BlockSpec((tm, UQ_W), lambda i: (i, 0)),
        out_shape=jax.ShapeDtypeStruct((t, UQ_W), BF16),
        compiler_params=_cparams(("parallel",)),
        name="dsa_query",
    )(p32, q_norm, w_uq16)


def _dsa_kernel(q_ref, qi_ref, small_ref, kidx_ref, k_ref, v_ref, o_ref, keys_ref,
                *, tq, kc, topk):
    i = pl.program_id(1)
    nkv = ((i + 1) * tq + kc - 1) // kc
    qpos = i * tq + lax.broadcasted_iota(I32, (1, tq), 1)
    kofs = lax.broadcasted_iota(I32, (kc, 1), 0)
    w_t = jnp.transpose(small_ref[...] * (IDX_HEADS ** -0.5) * (IDX_DIM ** -0.5))
    wrows = [w_t[SMALL_IDXW + h:SMALL_IDXW + h + 1, :] for h in range(IDX_HEADS)]
    qis = [qi_ref[:, h * LANES:(h + 1) * LANES] for h in range(IDX_HEADS)]

    def score_chunk(j, _):
        ks = pl.multiple_of(j * kc, kc)
        kix = kidx_ref[pl.ds(ks, kc), :]
        acc = jnp.zeros((kc, tq), F32)
        for h in range(IDX_HEADS):
            acc = acc + jnp.maximum(_dot_nt(kix, qis[h]), 0.0) * wrows[h]
        bits = pltpu.bitcast(acc, I32)
        key = jnp.where(bits < 0, bits ^ 0x7FFFFFFF, bits)
        causal = (j * kc + kofs) <= qpos
        keys_ref[j] = jnp.where(causal, key, INT_MIN)
        return 0

    lax.fori_loop(0, nkv, score_chunk, 0)

    def count_ge(cand):
        def cbody(j, part):
            hit = jnp.where(keys_ref[j] >= cand, 1.0, 0.0).reshape(kc // SUBLANES, SUBLANES, tq)
            return part + jnp.sum(hit, axis=0)
        part = lax.fori_loop(0, nkv, cbody, jnp.zeros((SUBLANES, tq), F32))
        return jnp.sum(part, axis=0, keepdims=True)

    def bit_step(it, theta):
        cand = theta + lax.shift_left(jnp.int32(1), 31 - it)
        return jnp.where(count_ge(cand) >= topk, cand, theta)

    theta = lax.fori_loop(0, 32, bit_step, jnp.full((1, tq), INT_MIN, I32))
    n_gt = jnp.where(theta == 2147483647, 0.0, count_ge(theta + 1))
    need = jnp.where(theta == INT_MIN, 0.0, topk - n_gt)

    ur = lax.broadcasted_iota(I32, (kc, kc), 0)
    uc = lax.broadcasted_iota(I32, (kc, kc), 1)
    upto = jnp.where(ur >= uc, 1.0, 0.0).astype(BF16)
    qs = [q_ref[:, h * HEAD_DIM:(h + 1) * HEAD_DIM] for h in range(N_HEADS)]
    scale = HEAD_DIM ** -0.5
    ninf = -jnp.inf

    def attend(j, carry):
        seen, ms, ls, accs = carry
        ks = pl.multiple_of(j * kc, kc)
        key = keys_ref[j]
        eq = key == theta
        pc = _dot(upto, jnp.where(eq, 1.0, 0.0).astype(BF16)) + seen
        rank = jnp.where(eq, pc - need, jnp.where(key > theta, -1.0, 1.0))
        sel = rank <= 0.0
        seen = pc[kc - 1:kc, :]
        kblk = k_ref[pl.ds(ks, kc), :]
        v_t = _bf(jnp.transpose(v_ref[pl.ds(ks, kc), :].astype(F32)))
        ms2, ls2, accs2 = [], [], []
        for h in range(N_HEADS):
            logit = jnp.where(sel, _dot_nt(kblk, qs[h]) * scale, ninf)
            m_new = jnp.maximum(ms[h], jnp.max(logit, axis=0, keepdims=True))
            p = jnp.exp(logit - m_new)
            a = jnp.exp(ms[h] - m_new)
            ls2.append(a * ls[h] + jnp.sum(p, axis=0, keepdims=True))
            accs2.append(a * accs[h] + _dot(v_t, _bf(p)))
            ms2.append(m_new)
        return seen, tuple(ms2), tuple(ls2), tuple(accs2)

    init = (jnp.zeros((1, tq), F32),
            tuple(jnp.full((1, tq), -1e30, F32) for _ in range(N_HEADS)),
            tuple(jnp.zeros((1, tq), F32) for _ in range(N_HEADS)),
            tuple(jnp.zeros((HEAD_DIM, tq), F32) for _ in range(N_HEADS)))
    _, _, ls, accs = lax.fori_loop(0, nkv, attend, init)
    for h in range(N_HEADS):
        o_ref[:, h * HEAD_DIM:(h + 1) * HEAD_DIM] = jnp.transpose(accs[h] / ls[h]).astype(o_ref.dtype)


def _dsa_attention(q_all, p32, p16, batch, seq):
    tq = min(256, seq)
    kc = min(256, seq)
    nq = seq // tq
    t = batch * seq
    topk = min(DSA_TOPK, seq // 4)
    kv = lambda blk: pl.BlockSpec((seq, LANES), lambda b, i, blk=blk: (b, blk))
    return pl.pallas_call(
        functools.partial(_dsa_kernel, tq=tq, kc=kc, topk=topk),
        grid=(batch, nq),
        in_specs=[pl.BlockSpec((tq, BRANCH_W), lambda b, i: (b * nq + i, UQ_Q_BLK)),
                  pl.BlockSpec((tq, IDX_HEADS * LANES), lambda b, i: (b * nq + i, 0)),
                  pl.BlockSpec((tq, LANES), lambda b, i: (b * nq + i, P32_SMALL_BLK)),
                  kv(P16_IDXK_BLK), kv(P16_DSA_K_BLK), kv(P16_DSA_V_BLK)],
        out_specs=pl.BlockSpec((tq, BRANCH_W), lambda b, i: (b * nq + i, 0)),
        out_shape=jax.ShapeDtypeStruct((t, BRANCH_W), BF16),
        scratch_shapes=[pltpu.VMEM((seq // kc, kc, tq), I32)],
        compiler_params=_cparams(("parallel", "arbitrary")),
        name="dsa_attention",
    )(q_all, q_all, p32, p16, p16, p16)


def _conv_kernel(prev_ref, x_ref, w_ref, o_ref, *, tl):
    first = pl.program_id(1) == 0
    prev = jnp.where(first, 0.0, prev_ref[...])
    ext = jnp.concatenate([prev, x_ref[...]], axis=0)
    acc = ext[SUBLANES:, :] * w_ref[GDN_CONV - 1:GDN_CONV, :]
    for d in range(1, GDN_CONV):
        acc = acc + pltpu.roll(ext, d, 0)[SUBLANES:, :] * w_ref[GDN_CONV - 1 - d:GDN_CONV - d, :]
    y = _silu(acc)
    is_v = pl.program_id(2) == 2
    for h in range(N_HEADS):
        hs = slice(h * HEAD_DIM, (h + 1) * HEAD_DIM)
        yh = y[:, hs]
        nrm = lax.rsqrt(jnp.sum(yh * yh, axis=-1, keepdims=True) + 1e-6)
        o_ref[:, hs] = yh * jnp.where(is_v, 1.0, nrm)


def _gdn_conv(p32, conv_w, batch, seq):
    tl = min(512, seq)
    nl = seq // tl
    t = batch * seq
    per8 = tl // SUBLANES
    return pl.pallas_call(
        functools.partial(_conv_kernel, tl=tl),
        grid=(batch, nl, 3),
        in_specs=[pl.BlockSpec((SUBLANES, BRANCH_W),
                               lambda b, i, c: (jnp.maximum((b * nl + i) * per8 - 1, 0), c)),
                  pl.BlockSpec((tl, BRANCH_W), lambda b, i, c: (b * nl + i, c)),
                  pl.BlockSpec((GDN_CONV, BRANCH_W), lambda b, i, c: (0, c))],
        out_specs=pl.BlockSpec((tl, BRANCH_W), lambda b, i, c: (b * nl + i, c)),
        out_shape=jax.ShapeDtypeStruct((t, 3 * BRANCH_W), F32),
        compiler_params=_cparams(("parallel", "parallel", "arbitrary")),
        name="gdn_conv",
    )(p32, p32, conv_w)


def _gdn_kernel(q_ref, k_ref, v_ref, gate_ref, small_ref, prm_ref, nrm_ref, o_ref, state_ref,
                *, n_chunks):
    c = GDN_CHUNK

    @pl.when(pl.program_id(1) == 0)
    def _():
        state_ref[...] = jnp.zeros_like(state_ref)

    lb = n_chunks * c
    g4 = N_HEADS * c
    ri = lax.broadcasted_iota(I32, (g4, g4), 0)
    cj = lax.broadcasted_iota(I32, (g4, g4), 1)
    sh = c.bit_length() - 1
    same = (ri >> sh) == (cj >> sh)
    incl = same & (ri >= cj)
    strict = same & (ri > cj)
    eye = jnp.where(ri == cj, 1.0, 0.0)
    li = lax.broadcasted_iota(I32, (lb, lb), 0)
    lj = lax.broadcasted_iota(I32, (lb, lb), 1)
    tri_chunks = jnp.where(((li >> sh) == (lj >> sh)) & (li >= lj), 1.0, 0.0).astype(BF16)

    small = small_ref[...]
    beta_all = _sigmoid(small)
    g_all = -jnp.exp(prm_ref[0:1, :]) * _softplus(small + prm_ref[1:2, :])
    gc_all = _dot_left_exact(tri_chunks, g_all)

    def stack(fn):
        return jnp.concatenate([fn(h) for h in range(N_HEADS)], axis=0)

    hsl = lambda h: slice(h * HEAD_DIM, (h + 1) * HEAD_DIM)
    scale = HEAD_DIM ** -0.5
    prepped = []
    for ci in range(n_chunks):
        rs = slice(ci * c, (ci + 1) * c)
        last = slice((ci + 1) * c - 1, (ci + 1) * c)
        k = stack(lambda h: k_ref[rs, hsl(h)])
        q = stack(lambda h: q_ref[rs, hsl(h)]) * scale
        v = stack(lambda h: v_ref[rs, hsl(h)])
        beta = stack(lambda h: beta_all[rs, SMALL_B + h:SMALL_B + h + 1])
        gc = stack(lambda h: gc_all[rs, SMALL_A + h:SMALL_A + h + 1])
        gc_last = stack(lambda h: jnp.broadcast_to(gc_all[last, SMALL_A + h:SMALL_A + h + 1], (c, 1)))
        gc_row = jnp.transpose(jnp.broadcast_to(gc, (g4, LANES)))[0:1, :]
        decay = jnp.where(incl, jnp.exp(jnp.where(incl, gc - gc_row, 0.0)), 0.0)
        k16 = _bf(k)
        k_beta = k * beta
        a = jnp.where(strict, _dot_nt(_bf(k_beta), k16) * decay, 0.0)
        nk = -a
        inv = eye + nk
        for _i in range(5):
            nk16 = _bf(nk)
            nk = _dot(nk16, nk16)
            inv = inv + _dot(_bf(inv), _bf(nk))
        e_gc = jnp.exp(gc)
        sol = _dot_hp(inv, jnp.concatenate([v * beta, k_beta * e_gc], axis=-1))
        qk = jnp.where(incl, _dot_nt(_bf(q), k16) * decay, 0.0)
        prepped.append(dict(u=sol[:, :HEAD_DIM], w16=_bf(sol[:, HEAD_DIM:]), qk16=_bf(qk),
                            qd16=_bf(q * e_gc), kd16=_bf(k * jnp.exp(gc_last - gc)),
                            cdec=[jnp.exp(gc_all[last, SMALL_A + h:SMALL_A + h + 1]) for h in range(N_HEADS)]))

    states = [state_ref[h] for h in range(N_HEADS)]
    for ci, p in enumerate(prepped):
        rs = slice(ci * c, (ci + 1) * c)
        row = lambda h: slice(h * c, (h + 1) * c)
        s16 = [_bf(s) for s in states]
        vn16 = _bf(stack(lambda h: p['u'][row(h)] - _dot(p['w16'][row(h)], s16[h])))
        o = _dot(p['qk16'], vn16) + stack(lambda h: _dot(p['qd16'][row(h)], s16[h]))
        states = [states[h] * p['cdec'][h] + _dot_tn(p['kd16'][row(h)], vn16[row(h)])
                  for h in range(N_HEADS)]
        for h in range(N_HEADS):
            oh = o[row(h)]
            ms = jnp.mean(oh * oh, axis=-1, keepdims=True)
            y = oh * lax.rsqrt(ms + NORM_EPS) * nrm_ref[...]
            o_ref[rs, hsl(h)] = (y * _silu(gate_ref[rs, hsl(h)])).astype(o_ref.dtype)
    for h in range(N_HEADS):
        state_ref[h] = states[h]


def _gated_deltanet(gq, p32, prm, gnorm, batch, seq):
    lb = min(512, seq)
    nb = seq // lb
    t = batch * seq
    blk = lambda cb: pl.BlockSpec((lb, BRANCH_W), lambda b, j, cb=cb: (b * nb + j, cb))
    return pl.pallas_call(
        functools.partial(_gdn_kernel, n_chunks=lb // GDN_CHUNK),
        grid=(batch, nb),
        in_specs=[blk(0), blk(1), blk(2), blk(7),
                  pl.BlockSpec((lb, LANES), lambda b, j: (b * nb + j, P32_SMALL_BLK)),
                  pl.BlockSpec((SUBLANES, LANES), lambda b, j: (0, 0)),
                  pl.BlockSpec((1, HEAD_DIM), lambda b, j: (0, 0))],
        out_specs=pl.BlockSpec((lb, BRANCH_W), lambda b, j: (b * nb + j, 0)),
        out_shape=jax.ShapeDtypeStruct((t, BRANCH_W), BF16),
        scratch_shapes=[pltpu.VMEM((N_HEADS, HEAD_DIM, HEAD_DIM), F32)],
        compiler_params=_cparams(("parallel", "arbitrary")),
        name="gated_deltanet",
    )(gq, gq, gq, p32, p32, prm, gnorm)


def _merge_kernel(x_ref, ya_ref, yb_ref, yc_ref, yd_ref, wg_ref, wb_ref, o_ref):
    x = x_ref[...]
    acc = None
    for i, y_ref in enumerate((ya_ref, yb_ref, yc_ref, yd_ref)):
        term = _sigmoid(_dot(x, wg_ref[i])) * _dot(y_ref[...], wb_ref[i])
        acc = term if acc is None else acc + term
    o_ref[...] = acc.astype(o_ref.dtype)


def _merge(x16, ys, wgl, wbl):
    (wg16, l), (wb16, _) = wgl, wbl
    t = x16.shape[0]
    tm = min(1024, t)
    tn = 256
    yspec = pl.BlockSpec((tm, BRANCH_W), lambda i, j: (i, 0))
    return pl.pallas_call(
        _merge_kernel,
        grid=(t // tm, D_MODEL // tn),
        in_specs=[pl.BlockSpec((tm, D_MODEL), lambda i, j: (i, 0)), yspec, yspec, yspec, yspec,
                  pl.BlockSpec((None, 4, D_MODEL, tn), lambda i, j: (l, 0, 0, j)),
                  pl.BlockSpec((None, 4, BRANCH_W, tn), lambda i, j: (l, 0, 0, j))],
        out_specs=pl.BlockSpec((tm, tn), lambda i, j: (i, j)),
        out_shape=jax.ShapeDtypeStruct((t, D_MODEL), BF16),
        compiler_params=_cparams(("parallel", "arbitrary")),
        name="gated_merge",
    )(x16, *ys, wg16, wb16)


def _layer_norm_rows(h, g, b):
    mu = jnp.mean(h, axis=-1, keepdims=True)
    d = h - mu
    var = jnp.mean(d * d, axis=-1, keepdims=True)
    return d * lax.rsqrt(var + NORM_EPS) * g + b


def _outproj_ln_kernel(m_ref, w_ref, x_ref, g_ref, b_ref, o32_ref, o16_ref):
    h = ALPHA * x_ref[...] + _dot(m_ref[...], w_ref[...])
    y = _layer_norm_rows(h, g_ref[...], b_ref[...])
    o32_ref[...] = y
    o16_ref[...] = _bf(y)


def _outproj_ln(merged16, woutl, x32, g, b):
    w_out16, l = woutl
    t = x32.shape[0]
    tm = min(512, t)
    row = pl.BlockSpec((tm, D_MODEL), lambda i: (i, 0))
    vec = pl.BlockSpec((1, D_MODEL), lambda i: (0, 0))
    return pl.pallas_call(
        _outproj_ln_kernel,
        grid=(t // tm,),
        in_specs=[row, pl.BlockSpec((None, D_MODEL, D_MODEL), lambda i: (l, 0, 0)), row, vec, vec],
        out_specs=[row, row],
        out_shape=[jax.ShapeDtypeStruct((t, D_MODEL), F32), jax.ShapeDtypeStruct((t, D_MODEL), BF16)],
        compiler_params=_cparams(("parallel",)),
        name="outproj_ln",
    )(merged16, w_out16, x32, g, b)


def _router_kernel(x_ref, whi_ref, wlo_ref, b_ref, ids_ref, gates_ref, sizes_ref, run_ref):
    @pl.when(pl.program_id(0) == 0)
    def _():
        run_ref[...] = jnp.zeros_like(run_ref)

    xh, xl = _split2(x_ref[...])
    whi = whi_ref[...]
    logits = _dot(xh, whi) + _dot(xh, wlo_ref[...]) + _dot(xl, whi) + b_ref[...]
    tm = logits.shape[0]
    lane = lax.broadcasted_iota(I32, (tm, LANES), 1)
    ninf = -jnp.inf
    gl = jnp.where(lane < N_GROUPS, logits, ninf)
    gmax = jnp.max(gl, axis=-1, keepdims=True)
    grp = jnp.min(jnp.where(gl == gmax, lane, LANES), axis=-1, keepdims=True)
    p_grp = 1.0 / jnp.sum(jnp.where(lane < N_GROUPS, jnp.exp(logits - gmax), 0.0), axis=-1, keepdims=True)
    lo = N_GROUPS + grp * EXPERTS_PER_GROUP
    el = jnp.where((lane >= lo) & (lane < lo + EXPERTS_PER_GROUP), logits, ninf)
    v1 = jnp.max(el, axis=-1, keepdims=True)
    i1 = jnp.min(jnp.where(el == v1, lane, LANES), axis=-1, keepdims=True)
    el2 = jnp.where(lane == i1, ninf, el)
    v2 = jnp.max(el2, axis=-1, keepdims=True)
    i2 = jnp.min(jnp.where(el2 == v2, lane, LANES), axis=-1, keepdims=True)
    e21 = jnp.exp(v2 - v1)
    den = 1.0 / (1.0 + e21)
    g1 = p_grp * den
    g2 = p_grp * (e21 * den)
    oh = jnp.where(lane == i1, 1.0, jnp.where(lane == i2, 1.0, 0.0))
    tr = lax.broadcasted_iota(I32, (tm, tm), 0)
    tc = lax.broadcasted_iota(I32, (tm, tm), 1)
    earlier = jnp.where(tr > tc, 1.0, 0.0).astype(BF16)
    before = _dot(earlier, _bf(oh)) + run_ref[0:1, :]
    r1 = jnp.sum(jnp.where(lane == i1, before, 0.0), axis=-1, keepdims=True).astype(I32)
    r2 = jnp.sum(jnp.where(lane == i2, before, 0.0), axis=-1, keepdims=True).astype(I32)
    run_new = before[tm - 1:tm, :] + oh[tm - 1:tm, :]
    run_ref[0:1, :] = run_new
    sizes_ref[...] = jnp.broadcast_to(run_new, sizes_ref.shape)
    ids_ref[...] = jnp.where(lane == 0, i1 - N_GROUPS, jnp.where(lane == 1, i2 - N_GROUPS,
                             jnp.where(lane == 2, r1, jnp.where(lane == 3, r2, 0))))
    gates_ref[...] = jnp.where(lane == 0, g1, jnp.where(lane == 1, g2, 0.0))


def _router(x32, whi, wlo, bias):
    t = x32.shape[0]
    tm = min(512, t)
    row = pl.BlockSpec((tm, LANES), lambda i: (i, 0))
    wsp = pl.BlockSpec((D_MODEL, LANES), lambda i: (0, 0))
    return pl.pallas_call(
        _router_kernel,
        grid=(t // tm,),
        in_specs=[pl.BlockSpec((tm, D_MODEL), lambda i: (i, 0)), wsp, wsp,
                  pl.BlockSpec((1, LANES), lambda i: (0, 0))],
        out_specs=[row, row, pl.BlockSpec((SUBLANES, LANES), lambda i: (0, 0))],
        out_shape=[jax.ShapeDtypeStruct((t, LANES), I32), jax.ShapeDtypeStruct((t, LANES), F32),
                   jax.ShapeDtypeStruct((SUBLANES, LANES), F32)],
        scratch_shapes=[pltpu.VMEM((SUBLANES, LANES), F32)],
        compiler_params=_cparams(("arbitrary",)),
        name="moe_router",
    )(x32, whi, wlo, bias)


DMA_UNROLL = 8


GATHER_DEPTH = 3


def _expert_kernel(be_ref, nb_ref, tok_ref, tok_next_ref, tok_next2_ref, x_hbm, wg_ref, wu_ref, wd_ref,
                   o_ref, xbuf, wg16, wu16, wd16, sem, *, blk):
    i = pl.program_id(0)
    nb = nb_ref[0]
    par = lax.rem(i, GATHER_DEPTH)

    def row_copy(r, tok, p):
        return pltpu.make_async_copy(x_hbm.at[pl.ds(tok, 1), :], xbuf.at[p, pl.ds(r, 1), :], sem.at[p])

    def issue(tref, p):
        def body(g, _):
            for u in range(DMA_UNROLL):
                r = g * DMA_UNROLL + u
                row_copy(r, tref[0, 0, r], p).start(priority=u % 2)
            return 0
        lax.fori_loop(0, blk // DMA_UNROLL, body, 0)

    @pl.when((i == 0) | (be_ref[i] != be_ref[jnp.maximum(i - 1, 0)]))
    def _():
        wg16[...] = _bf(wg_ref[...])
        wu16[...] = _bf(wu_ref[...])
        wd16[...] = _bf(wd_ref[...])

    @pl.when(i == 0)
    def _():
        issue(tok_ref, 0)

        @pl.when(1 < nb)
        def _():
            issue(tok_next_ref, 1)

    @pl.when(i + 2 < nb)
    def _():
        issue(tok_next2_ref, lax.rem(i + 2, GATHER_DEPTH))

    @pl.when(i < nb)
    def _():
        def wait(g, _):
            for u in range(DMA_UNROLL):
                row_copy(0, 0, par).wait()
            return 0
        lax.fori_loop(0, blk // DMA_UNROLL, wait, 0)
        xe = _bf(xbuf[par])
        h = _silu(_dot(xe, wg16[...])) * _dot(xe, wu16[...])
        o_ref[...] = _dot(_bf(h), wd16[...])

    @pl.when(i >= nb)
    def _():
        o_ref[...] = jnp.zeros_like(o_ref)


def _experts(x32, slot_tok, block_e, nb_used, wgl, wul, wdl, blk):
    (wg16, l), (wu16, _), (wd16, _) = wgl, wul, wdl
    n_blocks = block_e.shape[0]
    wspec = lambda shape: pl.BlockSpec((None, None) + shape, lambda i, be, nb: (l, be[i], 0, 0))
    tok3 = slot_tok.reshape(n_blocks, 1, blk)
    grid_spec = pltpu.PrefetchScalarGridSpec(
        num_scalar_prefetch=2,
        grid=(n_blocks,),
        in_specs=[pl.BlockSpec((1, 1, blk), lambda i, be, nb: (i, 0, 0), memory_space=pltpu.SMEM),
                  pl.BlockSpec((1, 1, blk), lambda i, be, nb: (jnp.minimum(i + 1, n_blocks - 1), 0, 0),
                               memory_space=pltpu.SMEM),
                  pl.BlockSpec((1, 1, blk), lambda i, be, nb: (jnp.minimum(i + 2, n_blocks - 1), 0, 0),
                               memory_space=pltpu.SMEM),
                  pl.BlockSpec(memory_space=pl.ANY),
                  wspec((D_MODEL, D_EXPERT)), wspec((D_MODEL, D_EXPERT)), wspec((D_EXPERT, D_MODEL))],
        out_specs=pl.BlockSpec((blk, D_MODEL), lambda i, be, nb: (i, 0)),
        scratch_shapes=[pltpu.VMEM((GATHER_DEPTH, blk, D_MODEL), F32),
                        pltpu.VMEM((D_MODEL, D_EXPERT), BF16), pltpu.VMEM((D_MODEL, D_EXPERT), BF16),
                        pltpu.VMEM((D_EXPERT, D_MODEL), BF16), pltpu.SemaphoreType.DMA((GATHER_DEPTH,))],
    )
    return pl.pallas_call(
        functools.partial(_expert_kernel, blk=blk),
        grid_spec=grid_spec,
        out_shape=jax.ShapeDtypeStruct((n_blocks * blk, D_MODEL), F32),
        compiler_params=_cparams(("arbitrary",)),
        name="moe_experts",
    )(block_e, nb_used, tok3, tok3, tok3, x32, wg16, wu16, wd16)


def _combine_kernel(slot_ref, slot_next_ref, y_hbm, x_ref, gates_ref, g_ref, b_ref, o32_ref, o16_ref,
                    ybuf, sem, *, tm):
    i = pl.program_id(0)
    n = pl.num_programs(0)
    par = lax.rem(i, 2)

    def row_copy(r, k, slot, p):
        return pltpu.make_async_copy(y_hbm.at[pl.ds(slot, 1), :], ybuf.at[p, k, pl.ds(r, 1), :], sem.at[p])

    def issue(sref, p):
        def body(g, _):
            for u in range(DMA_UNROLL):
                r = g * DMA_UNROLL + u
                row_copy(r, 0, sref[0, 0, 2 * r], p).start(priority=0)
                row_copy(r, 1, sref[0, 0, 2 * r + 1], p).start(priority=1)
            return 0
        lax.fori_loop(0, tm // DMA_UNROLL, body, 0)

    @pl.when(i == 0)
    def _():
        issue(slot_ref, 0)

    @pl.when(i + 1 < n)
    def _():
        issue(slot_next_ref, 1 - par)

    def wait(g, _):
        for u in range(2 * DMA_UNROLL):
            row_copy(0, 0, 0, par).wait()
        return 0

    lax.fori_loop(0, tm // DMA_UNROLL, wait, 0)
    gates = gates_ref[...]
    y = ybuf[par, 0] * gates[:, 0:1] + ybuf[par, 1] * gates[:, 1:2]
    out = _layer_norm_rows(ALPHA * x_ref[...] + y, g_ref[...], b_ref[...])
    o32_ref[...] = out
    o16_ref[...] = _bf(out)


def _combine_ln(slots, yb, x32, gates, g, b):
    t = x32.shape[0]
    tm = min(256, t)
    n = t // tm
    row = pl.BlockSpec((tm, D_MODEL), lambda i: (i, 0))
    vec = pl.BlockSpec((1, D_MODEL), lambda i: (0, 0))
    slots3 = slots.reshape(n, 1, 2 * tm)
    return pl.pallas_call(
        functools.partial(_combine_kernel, tm=tm),
        grid=(n,),
        in_specs=[pl.BlockSpec((1, 1, 2 * tm), lambda i: (i, 0, 0), memory_space=pltpu.SMEM),
                  pl.BlockSpec((1, 1, 2 * tm), lambda i: (jnp.minimum(i + 1, n - 1), 0, 0),
                               memory_space=pltpu.SMEM),
                  pl.BlockSpec(memory_space=pl.ANY), row,
                  pl.BlockSpec((tm, LANES), lambda i: (i, 0)), vec, vec],
        out_specs=[row, row],
        out_shape=[jax.ShapeDtypeStruct((t, D_MODEL), F32), jax.ShapeDtypeStruct((t, D_MODEL), BF16)],
        scratch_shapes=[pltpu.VMEM((2, 2, tm, D_MODEL), F32), pltpu.SemaphoreType.DMA((2,))],
        compiler_params=_cparams(("arbitrary",)),
        name="moe_combine_ln",
    )(slots3, slots3, yb, x32, gates, g, b)


def _moe_plan(ids, sizes_row, blk):
    t = ids.shape[0]
    sizes = sizes_row[0, N_GROUPS:N_GROUPS + N_EXPERTS].astype(I32)
    padded = (sizes + blk - 1) // blk * blk
    pad_ends = jnp.cumsum(padded)
    pad_starts = pad_ends - padded
    slots = (pad_starts[ids[:, 0:2]] + ids[:, 2:4]).astype(I32)
    n_blocks = 2 * t // blk + N_EXPERTS
    tok_of = jnp.broadcast_to(jnp.arange(t, dtype=I32)[:, None], (t, 2))
    slot_tok = jnp.zeros((n_blocks * blk,), I32).at[slots.reshape(-1)].set(tok_of.reshape(-1))
    starts = jnp.arange(n_blocks, dtype=I32) * blk
    block_e = jnp.minimum(jnp.sum((pad_ends[None, :] <= starts[:, None]).astype(I32), axis=-1),
                          N_EXPERTS - 1).astype(I32)
    nb_used = (pad_ends[-1:] // blk).astype(I32)
    return slots, slot_tok, block_e, nb_used


MOE_BLK = 512
STACKED = ('w16', 'w32', 'wg', 'wb', 'wout', 'eg', 'eu', 'ed')
IN_TM = 1024
IN_TN16, IN_TN32 = 1024, 768


def _prep_weights(w_in, dsa_q_norm, w_uq, gdn_conv, gdn_a_log, gdn_dt_bias, gdn_norm, w_branch,
                  w_branch_gate, w_out, ln1_g, ln1_b, w_router_group, b_router_group,
                  w_router_expert, b_router_expert, w_exp_gate, w_exp_up, w_exp_down, ln2_g, ln2_b):
    depth = w_in.shape[0]
    zc = lambda n: jnp.zeros((depth, D_MODEL, n), F32)
    c = lambda a, b: w_in[..., a:b]
    w16 = jnp.concatenate([c(0, 1536), c(3968, 4224), c(4224, 4288), zc(P16_W - 1856)], -1).astype(BF16)
    w32 = jnp.concatenate([c(4296, 5832), c(1536, 3584), c(5840, 6352), c(3584, 3968),
                           c(4288, 4296), c(5832, 5836), c(5836, 5840), zc(LANES - 16)], -1).astype(BF16)
    qi = w_uq[..., BRANCH_W:].reshape(depth, DSA_Q_RANK, IDX_HEADS, IDX_DIM)
    qi = jnp.pad(qi, ((0, 0), (0, 0), (0, 0), (0, LANES - IDX_DIM))).reshape(depth, DSA_Q_RANK, IDX_HEADS * LANES)
    wuq = jnp.concatenate([qi, w_uq[..., :BRANCH_W]], -1).astype(BF16)
    wr = jnp.concatenate([w_router_group, w_router_expert,
                          zc(LANES - N_GROUPS - N_EXPERTS)], -1)
    rhi = wr.astype(BF16)
    rlo = (wr - rhi.astype(F32)).astype(BF16)
    rb = jnp.concatenate([b_router_group, b_router_expert,
                          jnp.zeros((depth, LANES - N_GROUPS - N_EXPERTS), F32)], -1)[:, None, :]
    prm = jnp.zeros((depth, SUBLANES, LANES), F32)
    prm = prm.at[:, 0, SMALL_A:SMALL_A + N_HEADS].set(gdn_a_log)
    prm = prm.at[:, 1, SMALL_A:SMALL_A + N_HEADS].set(gdn_dt_bias)
    return dict(
        w16=w16, w32=w32, q_norm=dsa_q_norm[:, None, :], wuq=wuq, conv=gdn_conv, prm=prm,
        gnorm=gdn_norm[:, None, :], wb=w_branch.astype(BF16), wg=w_branch_gate.astype(BF16),
        wout=w_out.astype(BF16), ln1g=ln1_g[:, None, :], ln1b=ln1_b[:, None, :], rhi=rhi, rlo=rlo, rb=rb,
        eg=w_exp_gate, eu=w_exp_up, ed=w_exp_down,
        ln2g=ln2_g[:, None, :], ln2b=ln2_b[:, None, :])


def _mixer(x16, lw, batch, seq):
    p16 = _matmul(x16, lw['w16'], BF16, IN_TM, IN_TN16, "in_proj_bf16")
    p32 = _matmul(x16, lw['w32'], F32, IN_TM, IN_TN32, "in_proj_f32")
    y_a = _sb_attention(p16, batch, seq)
    y_b = _retention(p32, batch, seq)
    q_all = _dsa_query(p32, lw['q_norm'], lw['wuq'])
    y_c = _dsa_attention(q_all, p32, p16, batch, seq)
    gq = _gdn_conv(p32, lw['conv'], batch, seq)
    y_d = _gated_deltanet(gq, p32, lw['prm'], lw['gnorm'], batch, seq)
    return _merge(x16, (y_a, y_b, y_c, y_d), lw['wg'], lw['wb'])


def _moe(x32, lw):
    ids, gates, sizes = _router(x32, lw['rhi'], lw['rlo'], lw['rb'])
    slots, slot_tok, block_e, nb_used = _moe_plan(ids, sizes, MOE_BLK)
    yb = _experts(x32, slot_tok, block_e, nb_used, lw['eg'], lw['eu'], lw['ed'], MOE_BLK)
    return _combine_ln(slots, yb, x32, gates, lw['ln2g'], lw['ln2b'])


def _layer_weights(ws, l):
    return {k: ((v, l) if k in STACKED else v[l]) for k, v in ws.items()}


def _layer(x32, x16, lw, batch, seq):
    merged = _mixer(x16, lw, batch, seq)
    x32, x16 = _outproj_ln(merged, lw['wout'], x32, lw['ln1g'], lw['ln1b'])
    return _moe(x32, lw)


def kernel(x, w_in, dsa_q_norm, w_uq, gdn_conv, gdn_a_log, gdn_dt_bias, gdn_norm, w_branch, w_branch_gate,
           w_out, ln1_g, ln1_b, w_router_group, b_router_group, w_router_expert, b_router_expert,
           w_exp_gate, w_exp_up, w_exp_down, ln2_g, ln2_b):
    batch, seq, d = x.shape
    assert d == D_MODEL and seq % RET_CHUNK == 0
    ws = _prep_weights(w_in, dsa_q_norm, w_uq, gdn_conv, gdn_a_log, gdn_dt_bias, gdn_norm, w_branch,
                       w_branch_gate, w_out, ln1_g, ln1_b, w_router_group, b_router_group,
                       w_router_expert, b_router_expert, w_exp_gate, w_exp_up, w_exp_down, ln2_g, ln2_b)
    x32 = x.reshape(batch * seq, d)
    x16 = x32.astype(BF16)
    for l in range(w_in.shape[0]):
        x32, x16 = _layer(x32, x16, _layer_weights(ws, l), batch, seq)
    return x32.reshape(batch, seq, d)
```

```python
import functools
import math

import numpy as np
import jax
import jax.numpy as jnp
from jax import lax
from jax.experimental import pallas as pl
from jax.experimental.pallas import tpu as pltpu

F32 = jnp.float32
BF16 = jnp.bfloat16
I32 = jnp.int32

D_MODEL = 2048
DEPTH = 4
HEAD_DIM = 128
N_HEADS = 4
BRANCH_W = N_HEADS * HEAD_DIM
RET_CHUNK = 128
DSA_Q_RANK = 384
DSA_TOPK = 256
IDX_HEADS = 8
IDX_DIM = 64
GDN_CONV = 4
GDN_CHUNK = 64
N_GROUPS = 4
EXPERTS_PER_GROUP = 8
N_EXPERTS = N_GROUPS * EXPERTS_PER_GROUP
D_EXPERT = 512
ALPHA = (2.0 * DEPTH) ** 0.25
NORM_EPS = 1e-5

LANES = 128
SUBLANES = 8
VMEM_LIMIT_BYTES = 56 * 1024 * 1024

P16_W = 2048
P16_DSA_K_BLK = 12
P16_DSA_V_BLK = 13
P16_IDXK_BLK = 14
P32_W = 4608
P32_SMALL_BLK = 35
SMALL_IDXW, SMALL_B, SMALL_A = 0, 8, 12
UQ_W = IDX_HEADS * LANES + BRANCH_W
UQ_Q_BLK = IDX_HEADS * LANES // BRANCH_W

INT_MIN = -2147483648


def _cparams(sem):
    return pltpu.CompilerParams(dimension_semantics=sem, vmem_limit_bytes=VMEM_LIMIT_BYTES)


def _bf(x):
    return x.astype(BF16)


def _dot(a, b):
    return jnp.dot(a, b, preferred_element_type=F32)


def _dot_nt(a, b):
    return lax.dot_general(a, b, (((1,), (1,)), ((), ())), preferred_element_type=F32)


def _dot_tn(a, b):
    return lax.dot_general(a, b, (((0,), (0,)), ((), ())), preferred_element_type=F32)


def _split2(x):
    hi = _bf(x)
    lo = _bf(x - hi.astype(F32))
    return hi, lo


def _split3(x):
    hi = _bf(x)
    r = x - hi.astype(F32)
    mid = _bf(r)
    lo = _bf(r - mid.astype(F32))
    return hi, mid, lo


def _dot_left_exact(m01, x):
    hi, mid, lo = _split3(x)
    return _dot(m01, hi) + _dot(m01, mid) + _dot(m01, lo)


def _dot_right_exact(x, m01):
    hi, mid, lo = _split3(x)
    return _dot(hi, m01) + _dot(mid, m01) + _dot(lo, m01)


def _dot_hp(a, b):
    ah, al = _split2(a)
    bh, bl = _split2(b)
    return _dot(ah, bh) + _dot(ah, bl) + _dot(al, bh)


def _silu(x):
    return x * (1.0 / (1.0 + jnp.exp(-x)))


def _sigmoid(x):
    return 1.0 / (1.0 + jnp.exp(-x))


def _softplus(x):
    return jnp.maximum(x, 0.0) + jnp.log1p(jnp.exp(-jnp.abs(x)))


def _mm_kernel(x_ref, w_ref, o_ref):
    o_ref[...] = _dot(x_ref[...], w_ref[...]).astype(o_ref.dtype)


def _matmul(x, wl, out_dtype, tm, tn, name):
    w, l = wl
    m, k = x.shape
    n = w.shape[2]
    tm = min(tm, m)
    return pl.pallas_call(
        _mm_kernel,
        grid=(m // tm, n // tn),
        in_specs=[pl.BlockSpec((tm, k), lambda i, j: (i, 0)),
                  pl.BlockSpec((None, k, tn), lambda i, j: (l, 0, j))],
        out_specs=pl.BlockSpec((tm, tn), lambda i, j: (i, j)),
        out_shape=jax.ShapeDtypeStruct((m, n), out_dtype),
        compiler_params=_cparams(("parallel", "arbitrary")),
        name=name,
    )(x, w)


def _sb_kernel(q_ref, k_ref, v_ref, o_ref, *, tq):
    i = pl.program_id(1)
    scale = HEAD_DIM ** -0.5
    hsl = lambda h: slice(h * HEAD_DIM, (h + 1) * HEAD_DIM)
    qs = [q_ref[:, hsl(h)] for h in range(N_HEADS)]
    jr = lax.broadcasted_iota(I32, (tq, tq), 0)
    jc = lax.broadcasted_iota(I32, (tq, tq), 1)
    below = jr > jc
    later = jnp.where(below, 1.0, 0.0).astype(BF16)

    def block(kb, carry, diagonal):
        ks = pl.multiple_of(kb * tq, tq)
        out = []
        for h in range(N_HEADS):
            acc, run = carry[h]
            kblk = k_ref[pl.ds(ks, tq), hsl(h)]
            vblk = v_ref[pl.ds(ks, tq), hsl(h)]
            z = _dot_nt(qs[h], kblk) * scale
            sp = jnp.log(1.0 + jnp.exp(-jnp.abs(z)))
            log_beta = jnp.minimum(z, 0.0) - sp
            log_1m = log_beta - z
            if diagonal:
                log_1m = jnp.where(below, log_1m, 0.0)
            suffix = _dot(_bf(log_1m), later)
            w = jnp.exp(log_beta + suffix + run)
            if diagonal:
                w = jnp.where(below, w, 0.0)
            out.append((acc + _dot(_bf(w), vblk), run + suffix[:, 0:1] + log_1m[:, 0:1]))
        return tuple(out)

    init = tuple((jnp.zeros((tq, HEAD_DIM), F32), jnp.zeros((tq, 1), F32)) for _ in range(N_HEADS))
    res = block(i, init, True)

    def pair(jj, carry):
        kb = i - 1 - 2 * jj
        return block(kb - 1, block(kb, carry, False), False)

    res = lax.fori_loop(0, i // 2, pair, res)
    res = lax.cond(lax.rem(i, 2) == 1, lambda c: block(0, c, False), lambda c: c, res)
    for h in range(N_HEADS):
        o_ref[:, hsl(h)] = res[h][0].astype(o_ref.dtype)


def _sb_attention(p16, batch, seq):
    tq = min(256, seq)
    nq = seq // tq
    t = batch * seq
    return pl.pallas_call(
        functools.partial(_sb_kernel, tq=tq),
        grid=(batch, nq),
        in_specs=[pl.BlockSpec((tq, BRANCH_W), lambda b, i: (b * nq + i, 0)),
                  pl.BlockSpec((seq, BRANCH_W), lambda b, i: (b, 1)),
                  pl.BlockSpec((seq, BRANCH_W), lambda b, i: (b, 2))],
        out_specs=pl.BlockSpec((tq, BRANCH_W), lambda b, i: (b * nq + i, 0)),
        out_shape=jax.ShapeDtypeStruct((t, BRANCH_W), BF16),
        compiler_params=_cparams(("parallel", "arbitrary")),
        name="sb_attention",
    )(p16, p16, p16)


def _ret_kernel(dch_ref, q_ref, k_ref, v_ref, g_ref, cos_ref, sin_ref, dintra_ref, dq_ref, dk_ref,
                o_ref, state_ref, *, n_chunks):
    c = RET_CHUNK

    @pl.when(pl.program_id(1) == 0)
    def _():
        state_ref[...] = jnp.zeros_like(state_ref)

    def rot(x, cs, sn):
        return x * cs + pltpu.roll(x, HEAD_DIM // 2, 1) * sn

    for h in range(N_HEADS):
        hs = slice(h * HEAD_DIM, (h + 1) * HEAD_DIM)
        state = state_ref[h]
        for ci in range(n_chunks):
            rs = slice(ci * c, (ci + 1) * c)
            cs = cos_ref[rs, :]
            sn = sin_ref[rs, :]
            qc = rot(q_ref[rs, hs], cs, sn)
            kc = rot(k_ref[rs, hs], cs, sn) * (HEAD_DIM ** -0.5)
            vc = _bf(v_ref[rs, hs])
            qb = _bf(qc)
            s = _dot_nt(qb, _bf(kc)) * dintra_ref[h]
            o = _dot(_bf(s), vc) + _dot(qb, _bf(state)) * dq_ref[h]
            kv = _dot_tn(_bf(kc * dk_ref[h]), vc)
            state = dch_ref[h] * state + kv
            mu = jnp.mean(o, axis=-1, keepdims=True)
            d = o - mu
            var = jnp.mean(d * d, axis=-1, keepdims=True)
            y = d * lax.rsqrt(var + NORM_EPS)
            o_ref[rs, hs] = (y * _silu(g_ref[rs, hs])).astype(o_ref.dtype)
        state_ref[h] = state


def _retention(p32, batch, seq):
    lb = min(512, seq)
    nb = seq // lb
    t = batch * seq
    f32 = F32
    hh = N_HEADS
    c = RET_CHUNK
    log_gamma = jnp.log1p(-jnp.exp2(-5.0 - jnp.arange(hh, dtype=f32)))
    pos = jnp.arange(c, dtype=f32)
    rel = pos[:, None] - pos[None, :]
    causal = rel >= 0
    d_intra = jnp.where(causal, jnp.exp(jnp.where(causal, rel, 0.0) * log_gamma[:, None, None]), 0.0)
    d_q = jnp.exp((pos + 1.0) * log_gamma[:, None])
    d_k = jnp.exp((c - 1.0 - pos) * log_gamma[:, None])
    d_chunk = jnp.exp(c * log_gamma)
    dq_b = jnp.broadcast_to(d_q[:, :, None], (hh, c, HEAD_DIM))
    dk_b = jnp.broadcast_to(d_k[:, :, None], (hh, c, HEAD_DIM))
    inv = 1.0 / (10000.0 ** (jnp.arange(0, HEAD_DIM, 2, dtype=f32) / HEAD_DIM))
    ang = jnp.arange(seq).astype(f32)[:, None] * inv[None, :]
    cos_t = jnp.concatenate([jnp.cos(ang), jnp.cos(ang)], -1)
    sin_t = jnp.concatenate([-jnp.sin(ang), jnp.sin(ang)], -1)

    qkvg = lambda blk: pl.BlockSpec((lb, BRANCH_W), lambda b, j, blk=blk: (b * nb + j, blk))
    tab = pl.BlockSpec((lb, HEAD_DIM), lambda b, j: (j, 0))
    whole3 = pl.BlockSpec((hh, c, HEAD_DIM), lambda b, j: (0, 0, 0))
    return pl.pallas_call(
        functools.partial(_ret_kernel, n_chunks=lb // c),
        grid=(batch, nb),
        in_specs=[pl.BlockSpec(memory_space=pltpu.SMEM),
                  qkvg(3), qkvg(4), qkvg(5), qkvg(6), tab, tab, whole3, whole3, whole3],
        out_specs=pl.BlockSpec((lb, BRANCH_W), lambda b, j: (b * nb + j, 0)),
        out_shape=jax.ShapeDtypeStruct((t, BRANCH_W), BF16),
        scratch_shapes=[pltpu.VMEM((hh, HEAD_DIM, HEAD_DIM), F32)],
        compiler_params=_cparams(("parallel", "arbitrary")),
        name="retention",
    )(d_chunk, p32, p32, p32, p32, cos_t, sin_t, d_intra, dq_b, dk_b)


def _uq_kernel(x_ref, g_ref, w_ref, o_ref):
    x = x_ref[:, :DSA_Q_RANK]
    ms = jnp.mean(x * x, axis=-1, keepdims=True)
    xn = x * lax.rsqrt(ms + NORM_EPS) * g_ref[...]
    o_ref[...] = _dot(_bf(xn), w_ref[...]).astype(o_ref.dtype)


def _dsa_query(p32, q_norm, w_uq16):
    t = p32.shape[0]
    tm = min(512, t)
    return pl.pallas_call(
        _uq_kernel,
        grid=(t // tm,),
        in_specs=[pl.BlockSpec((tm, 512), lambda i: (i, 8)),
                  pl.BlockSpec((1, DSA_Q_RANK), lambda i: (0, 0)),
                  pl.BlockSpec((DSA_Q_RANK, UQ_W), lambda i: (0, 0))],
        out_specs=pl.BlockSpec((tm, UQ_W), lambda i: (i, 0)),
        out_shape=jax.ShapeDtypeStruct((t, UQ_W), BF16),
        compiler_params=_cparams(("parallel",)),
        name="dsa_query",
    )(p32, q_norm, w_uq16)


def _dsa_kernel(q_ref, qi_ref, small_ref, kidx_ref, k_ref, v_ref, o_ref, keys_ref,
                *, tq, kc, topk):
    i = pl.program_id(1)
    nkv = ((i + 1) * tq + kc - 1) // kc
    qpos = i * tq + lax.broadcasted_iota(I32, (1, tq), 1)
    kofs = lax.broadcasted_iota(I32, (kc, 1), 0)
    w_t = jnp.transpose(small_ref[...] * (IDX_HEADS ** -0.5) * (IDX_DIM ** -0.5))
    wrows = [w_t[SMALL_IDXW + h:SMALL_IDXW + h + 1, :] for h in range(IDX_HEADS)]
    qis = [qi_ref[:, h * LANES:(h + 1) * LANES] for h in range(IDX_HEADS)]

    def score_chunk(j, _):
        ks = pl.multiple_of(j * kc, kc)
        kix = kidx_ref[pl.ds(ks, kc), :]
        acc = jnp.zeros((kc, tq), F32)
        for h in range(IDX_HEADS):
            acc = acc + jnp.maximum(_dot_nt(kix, qis[h]), 0.0) * wrows[h]
        bits = pltpu.bitcast(acc, I32)
        key = jnp.where(bits < 0, bits ^ 0x7FFFFFFF, bits)
        causal = (j * kc + kofs) <= qpos
        keys_ref[j] = jnp.where(causal, key, INT_MIN)
        return 0

    lax.fori_loop(0, nkv, score_chunk, 0)

    def count_ge(cand):
        def cbody(j, part):
            hit = jnp.where(keys_ref[j] >= cand, 1.0, 0.0).reshape(kc // SUBLANES, SUBLANES, tq)
            return part + jnp.sum(hit, axis=0)
        part = lax.fori_loop(0, nkv, cbody, jnp.zeros((SUBLANES, tq), F32))
        return jnp.sum(part, axis=0, keepdims=True)

    def bit_step(it, theta):
        cand = theta + lax.shift_left(jnp.int32(1), 31 - it)
        return jnp.where(count_ge(cand) >= topk, cand, theta)

    theta = lax.fori_loop(0, 32, bit_step, jnp.full((1, tq), INT_MIN, I32))
    n_gt = jnp.where(theta == 2147483647, 0.0, count_ge(theta + 1))
    need = jnp.where(theta == INT_MIN, 0.0, topk - n_gt)

    ur = lax.broadcasted_iota(I32, (kc, kc), 0)
    uc = lax.broadcasted_iota(I32, (kc, kc), 1)
    upto = jnp.where(ur >= uc, 1.0, 0.0).astype(BF16)
    qs = [q_ref[:, h * HEAD_DIM:(h + 1) * HEAD_DIM] for h in range(N_HEADS)]
    scale = HEAD_DIM ** -0.5
    ninf = -jnp.inf

    def attend(j, carry):
        seen, ms, ls, accs = carry
        ks = pl.multiple_of(j * kc, kc)
        key = keys_ref[j]
        eq = key == theta
        pc = _dot(upto, jnp.where(eq, 1.0, 0.0).astype(BF16)) + seen
        rank = jnp.where(eq, pc - need, jnp.where(key > theta, -1.0, 1.0))
        sel = rank <= 0.0
        seen = pc[kc - 1:kc, :]
        kblk = k_ref[pl.ds(ks, kc), :]
        v_t = _bf(jnp.transpose(v_ref[pl.ds(ks, kc), :].astype(F32)))
        ms2, ls2, accs2 = [], [], []
        for h in range(N_HEADS):
            logit = jnp.where(sel, _dot_nt(kblk, qs[h]) * scale, ninf)
            m_new = jnp.maximum(ms[h], jnp.max(logit, axis=0, keepdims=True))
            p = jnp.exp(logit - m_new)
            a = jnp.exp(ms[h] - m_new)
            ls2.append(a * ls[h] + jnp.sum(p, axis=0, keepdims=True))
            accs2.append(a * accs[h] + _dot(v_t, _bf(p)))
            ms2.append(m_new)
        return seen, tuple(ms2), tuple(ls2), tuple(accs2)

    init = (jnp.zeros((1, tq), F32),
            tuple(jnp.full((1, tq), -1e30, F32) for _ in range(N_HEADS)),
            tuple(jnp.zeros((1, tq), F32) for _ in range(N_HEADS)),
            tuple(jnp.zeros((HEAD_DIM, tq), F32) for _ in range(N_HEADS)))
    _, _, ls, accs = lax.fori_loop(0, nkv, attend, init)
    for h in range(N_HEADS):
        o_ref[:, h * HEAD_DIM:(h + 1) * HEAD_DIM] = jnp.transpose(accs[h] / ls[h]).astype(o_ref.dtype)


def _dsa_attention(q_all, p32, p16, batch, seq):
    tq = min(256, seq)
    kc = min(256, seq)
    nq = seq // tq
    t = batch * seq
    topk = min(DSA_TOPK, seq // 4)
    kv = lambda blk: pl.BlockSpec((seq, LANES), lambda b, i, blk=blk: (b, blk))
    return pl.pallas_call(
        functools.partial(_dsa_kernel, tq=tq, kc=kc, topk=topk),
        grid=(batch, nq),
        in_specs=[pl.BlockSpec((tq, BRANCH_W), lambda b, i: (b * nq + i, UQ_Q_BLK)),
                  pl.BlockSpec((tq, IDX_HEADS * LANES), lambda b, i: (b * nq + i, 0)),
                  pl.BlockSpec((tq, LANES), lambda b, i: (b * nq + i, P32_SMALL_BLK)),
                  kv(P16_IDXK_BLK), kv(P16_DSA_K_BLK), kv(P16_DSA_V_BLK)],
        out_specs=pl.BlockSpec((tq, BRANCH_W), lambda b, i: (b * nq + i, 0)),
        out_shape=jax.ShapeDtypeStruct((t, BRANCH_W), BF16),
        scratch_shapes=[pltpu.VMEM((seq // kc, kc, tq), I32)],
        compiler_params=_cparams(("parallel", "arbitrary")),
        name="dsa_attention",
    )(q_all, q_all, p32, p16, p16, p16)


def _conv_kernel(prev_ref, x_ref, w_ref, o_ref, *, tl):
    first = pl.program_id(1) == 0
    prev = jnp.where(first, 0.0, prev_ref[...])
    ext = jnp.concatenate([prev, x_ref[...]], axis=0)
    acc = ext[SUBLANES:, :] * w_ref[GDN_CONV - 1:GDN_CONV, :]
    for d in range(1, GDN_CONV):
        acc = acc + pltpu.roll(ext, d, 0)[SUBLANES:, :] * w_ref[GDN_CONV - 1 - d:GDN_CONV - d, :]
    y = _silu(acc)
    is_v = pl.program_id(2) == 2
    for h in range(N_HEADS):
        hs = slice(h * HEAD_DIM, (h + 1) * HEAD_DIM)
        yh = y[:, hs]
        nrm = lax.rsqrt(jnp.sum(yh * yh, axis=-1, keepdims=True) + 1e-6)
        o_ref[:, hs] = yh * jnp.where(is_v, 1.0, nrm)


def _gdn_conv(p32, conv_w, batch, seq):
    tl = min(512, seq)
    nl = seq // tl
    t = batch * seq
    per8 = tl // SUBLANES
    return pl.pallas_call(
        functools.partial(_conv_kernel, tl=tl),
        grid=(batch, nl, 3),
        in_specs=[pl.BlockSpec((SUBLANES, BRANCH_W),
                               lambda b, i, c: (jnp.maximum((b * nl + i) * per8 - 1, 0), c)),
                  pl.BlockSpec((tl, BRANCH_W), lambda b, i, c: (b * nl + i, c)),
                  pl.BlockSpec((GDN_CONV, BRANCH_W), lambda b, i, c: (0, c))],
        out_specs=pl.BlockSpec((tl, BRANCH_W), lambda b, i, c: (b * nl + i, c)),
        out_shape=jax.ShapeDtypeStruct((t, 3 * BRANCH_W), F32),
        compiler_params=_cparams(("parallel", "parallel", "arbitrary")),
        name="gdn_conv",
    )(p32, p32, conv_w)


def _gdn_kernel(q_ref, k_ref, v_ref, gate_ref, small_ref, prm_ref, nrm_ref, o_ref, state_ref,
                *, n_chunks):
    c = GDN_CHUNK

    @pl.when(pl.program_id(1) == 0)
    def _():
        state_ref[...] = jnp.zeros_like(state_ref)

    lb = n_chunks * c
    g4 = N_HEADS * c
    ri = lax.broadcasted_iota(I32, (g4, g4), 0)
    cj = lax.broadcasted_iota(I32, (g4, g4), 1)
    sh = c.bit_length() - 1
    same = (ri >> sh) == (cj >> sh)
    incl = same & (ri >= cj)
    strict = same & (ri > cj)
    eye = jnp.where(ri == cj, 1.0, 0.0)
    li = lax.broadcasted_iota(I32, (lb, lb), 0)
    lj = lax.broadcasted_iota(I32, (lb, lb), 1)
    tri_chunks = jnp.where(((li >> sh) == (lj >> sh)) & (li >= lj), 1.0, 0.0).astype(BF16)

    small = small_ref[...]
    beta_all = _sigmoid(small)
    g_all = -jnp.exp(prm_ref[0:1, :]) * _softplus(small + prm_ref[1:2, :])
    gc_all = _dot_left_exact(tri_chunks, g_all)

    def stack(fn):
        return jnp.concatenate([fn(h) for h in range(N_HEADS)], axis=0)

    hsl = lambda h: slice(h * HEAD_DIM, (h + 1) * HEAD_DIM)
    scale = HEAD_DIM ** -0.5
    prepped = []
    for ci in range(n_chunks):
        rs = slice(ci * c, (ci + 1) * c)
        last = slice((ci + 1) * c - 1, (ci + 1) * c)
        k = stack(lambda h: k_ref[rs, hsl(h)])
        q = stack(lambda h: q_ref[rs, hsl(h)]) * scale
        v = stack(lambda h: v_ref[rs, hsl(h)])
        beta = stack(lambda h: beta_all[rs, SMALL_B + h:SMALL_B + h + 1])
        gc = stack(lambda h: gc_all[rs, SMALL_A + h:SMALL_A + h + 1])
        gc_last = stack(lambda h: jnp.broadcast_to(gc_all[last, SMALL_A + h:SMALL_A + h + 1], (c, 1)))
        gc_row = jnp.transpose(jnp.broadcast_to(gc, (g4, LANES)))[0:1, :]
        decay = jnp.where(incl, jnp.exp(jnp.where(incl, gc - gc_row, 0.0)), 0.0)
        k16 = _bf(k)
        k_beta = k * beta
        a = jnp.where(strict, _dot_nt(_bf(k_beta), k16) * decay, 0.0)
        nk = -a
        inv = eye + nk
        for _i in range(5):
            nk16 = _bf(nk)
            nk = _dot(nk16, nk16)
            inv = inv + _dot(_bf(inv), _bf(nk))
        e_gc = jnp.exp(gc)
        sol = _dot_hp(inv, jnp.concatenate([v * beta, k_beta * e_gc], axis=-1))
        qk = jnp.where(incl, _dot_nt(_bf(q), k16) * decay, 0.0)
        prepped.append(dict(u=sol[:, :HEAD_DIM], w16=_bf(sol[:, HEAD_DIM:]), qk16=_bf(qk),
                            qd16=_bf(q * e_gc), kd16=_bf(k * jnp.exp(gc_last - gc)),
                            cdec=[jnp.exp(gc_all[last, SMALL_A + h:SMALL_A + h + 1]) for h in range(N_HEADS)]))

    states = [state_ref[h] for h in range(N_HEADS)]
    for ci, p in enumerate(prepped):
        rs = slice(ci * c, (ci + 1) * c)
        row = lambda h: slice(h * c, (h + 1) * c)
        s16 = [_bf(s) for s in states]
        vn16 = _bf(stack(lambda h: p['u'][row(h)] - _dot(p['w16'][row(h)], s16[h])))
        o = _dot(p['qk16'], vn16) + stack(lambda h: _dot(p['qd16'][row(h)], s16[h]))
        states = [states[h] * p['cdec'][h] + _dot_tn(p['kd16'][row(h)], vn16[row(h)])
                  for h in range(N_HEADS)]
        for h in range(N_HEADS):
            oh = o[row(h)]
            ms = jnp.mean(oh * oh, axis=-1, keepdims=True)
            y = oh * lax.rsqrt(ms + NORM_EPS) * nrm_ref[...]
            o_ref[rs, hsl(h)] = (y * _silu(gate_ref[rs, hsl(h)])).astype(o_ref.dtype)
    for h in range(N_HEADS):
        state_ref[h] = states[h]


def _gated_deltanet(gq, p32, prm, gnorm, batch, seq):
    lb = min(512, seq)
    nb = seq // lb
    t = batch * seq
    blk = lambda cb: pl.BlockSpec((lb, BRANCH_W), lambda b, j, cb=cb: (b * nb + j, cb))
    return pl.pallas_call(
        functools.partial(_gdn_kernel, n_chunks=lb // GDN_CHUNK),
        grid=(batch, nb),
        in_specs=[blk(0), blk(1), blk(2), blk(7),
                  pl.BlockSpec((lb, LANES), lambda b, j: (b * nb + j, P32_SMALL_BLK)),
                  pl.BlockSpec((SUBLANES, LANES), lambda b, j: (0, 0)),
                  pl.BlockSpec((1, HEAD_DIM), lambda b, j: (0, 0))],
        out_specs=pl.BlockSpec((lb, BRANCH_W), lambda b, j: (b * nb + j, 0)),
        out_shape=jax.ShapeDtypeStruct((t, BRANCH_W), BF16),
        scratch_shapes=[pltpu.VMEM((N_HEADS, HEAD_DIM, HEAD_DIM), F32)],
        compiler_params=_cparams(("parallel", "arbitrary")),
        name="gated_deltanet",
    )(gq, gq, gq, p32, p32, prm, gnorm)


def _merge_kernel(x_ref, ya_ref, yb_ref, yc_ref, yd_ref, wg_ref, wb_ref, o_ref):
    x = x_ref[...]
    acc = None
    for i, y_ref in enumerate((ya_ref, yb_ref, yc_ref, yd_ref)):
        term = _sigmoid(_dot(x, wg_ref[i])) * _dot(y_ref[...], wb_ref[i])
        acc = term if acc is None else acc + term
    o_ref[...] = acc.astype(o_ref.dtype)


def _merge(x16, ys, wgl, wbl):
    (wg16, l), (wb16, _) = wgl, wbl
    t = x16.shape[0]
    tm = min(1024, t)
    tn = 256
    yspec = pl.BlockSpec((tm, BRANCH_W), lambda i, j: (i, 0))
    return pl.pallas_call(
        _merge_kernel,
        grid=(t // tm, D_MODEL // tn),
        in_specs=[pl.BlockSpec((tm, D_MODEL), lambda i, j: (i, 0)), yspec, yspec, yspec, yspec,
                  pl.BlockSpec((None, 4, D_MODEL, tn), lambda i, j: (l, 0, 0, j)),
                  pl.BlockSpec((None, 4, BRANCH_W, tn), lambda i, j: (l, 0, 0, j))],
        out_specs=pl.BlockSpec((tm, tn), lambda i, j: (i, j)),
        out_shape=jax.ShapeDtypeStruct((t, D_MODEL), BF16),
        compiler_params=_cparams(("parallel", "arbitrary")),
        name="gated_merge",
    )(x16, *ys, wg16, wb16)


def _layer_norm_rows(h, g, b):
    mu = jnp.mean(h, axis=-1, keepdims=True)
    d = h - mu
    var = jnp.mean(d * d, axis=-1, keepdims=True)
    return d * lax.rsqrt(var + NORM_EPS) * g + b


def _outproj_ln_kernel(m_ref, w_ref, x_ref, g_ref, b_ref, o32_ref, o16_ref):
    h = ALPHA * x_ref[...] + _dot(m_ref[...], w_ref[...])
    y = _layer_norm_rows(h, g_ref[...], b_ref[...])
    o32_ref[...] = y
    o16_ref[...] = _bf(y)


def _outproj_ln(merged16, woutl, x32, g, b):
    w_out16, l = woutl
    t = x32.shape[0]
    tm = min(512, t)
    row = pl.BlockSpec((tm, D_MODEL), lambda i: (i, 0))
    vec = pl.BlockSpec((1, D_MODEL), lambda i: (0, 0))
    return pl.pallas_call(
        _outproj_ln_kernel,
        grid=(t // tm,),
        in_specs=[row, pl.BlockSpec((None, D_MODEL, D_MODEL), lambda i: (l, 0, 0)), row, vec, vec],
        out_specs=[row, row],
        out_shape=[jax.ShapeDtypeStruct((t, D_MODEL), F32), jax.ShapeDtypeStruct((t, D_MODEL), BF16)],
        compiler_params=_cparams(("parallel",)),
        name="outproj_ln",
    )(merged16, w_out16, x32, g, b)


def _router_kernel(x_ref, whi_ref, wlo_ref, b_ref, ids_ref, gates_ref, sizes_ref, run_ref):
    @pl.when(pl.program_id(0) == 0)
    def _():
        run_ref[...] = jnp.zeros_like(run_ref)

    xh, xl = _split2(x_ref[...])
    whi = whi_ref[...]
    logits = _dot(xh, whi) + _dot(xh, wlo_ref[...]) + _dot(xl, whi) + b_ref[...]
    tm = logits.shape[0]
    lane = lax.broadcasted_iota(I32, (tm, LANES), 1)
    ninf = -jnp.inf
    gl = jnp.where(lane < N_GROUPS, logits, ninf)
    gmax = jnp.max(gl, axis=-1, keepdims=True)
    grp = jnp.min(jnp.where(gl == gmax, lane, LANES), axis=-1, keepdims=True)
    p_grp = 1.0 / jnp.sum(jnp.where(lane < N_GROUPS, jnp.exp(logits - gmax), 0.0), axis=-1, keepdims=True)
    lo = N_GROUPS + grp * EXPERTS_PER_GROUP
    el = jnp.where((lane >= lo) & (lane < lo + EXPERTS_PER_GROUP), logits, ninf)
    v1 = jnp.max(el, axis=-1, keepdims=True)
    i1 = jnp.min(jnp.where(el == v1, lane, LANES), axis=-1, keepdims=True)
    el2 = jnp.where(lane == i1, ninf, el)
    v2 = jnp.max(el2, axis=-1, keepdims=True)
    i2 = jnp.min(jnp.where(el2 == v2, lane, LANES), axis=-1, keepdims=True)
    e21 = jnp.exp(v2 - v1)
    den = 1.0 / (1.0 + e21)
    g1 = p_grp * den
    g2 = p_grp * (e21 * den)
    oh = jnp.where(lane == i1, 1.0, jnp.where(lane == i2, 1.0, 0.0))
    tr = lax.broadcasted_iota(I32, (tm, tm), 0)
    tc = lax.broadcasted_iota(I32, (tm, tm), 1)
    earlier = jnp.where(tr > tc, 1.0, 0.0).astype(BF16)
    before = _dot(earlier, _bf(oh)) + run_ref[0:1, :]
    r1 = jnp.sum(jnp.where(lane == i1, before, 0.0), axis=-1, keepdims=True).astype(I32)
    r2 = jnp.sum(jnp.where(lane == i2, before, 0.0), axis=-1, keepdims=True).astype(I32)
    run_new = before[tm - 1:tm, :] + oh[tm - 1:tm, :]
    run_ref[0:1, :] = run_new
    sizes_ref[...] = jnp.broadcast_to(run_new, sizes_ref.shape)
    ids_ref[...] = jnp.where(lane == 0, i1 - N_GROUPS, jnp.where(lane == 1, i2 - N_GROUPS,
                             jnp.where(lane == 2, r1, jnp.where(lane == 3, r2, 0))))
    gates_ref[...] = jnp.where(lane == 0, g1, jnp.where(lane == 1, g2, 0.0))


def _router(x32, whi, wlo, bias):
    t = x32.shape[0]
    tm = min(512, t)
    row = pl.BlockSpec((tm, LANES), lambda i: (i, 0))
    wsp = pl.BlockSpec((D_MODEL, LANES), lambda i: (0, 0))
    return pl.pallas_call(
        _router_kernel,
        grid=(t // tm,),
        in_specs=[pl.BlockSpec((tm, D_MODEL), lambda i: (i, 0)), wsp, wsp,
                  pl.BlockSpec((1, LANES), lambda i: (0, 0))],
        out_specs=[row, row, pl.BlockSpec((SUBLANES, LANES), lambda i: (0, 0))],
        out_shape=[jax.ShapeDtypeStruct((t, LANES), I32), jax.ShapeDtypeStruct((t, LANES), F32),
                   jax.ShapeDtypeStruct((SUBLANES, LANES), F32)],
        scratch_shapes=[pltpu.VMEM((SUBLANES, LANES), F32)],
        compiler_params=_cparams(("arbitrary",)),
        name="moe_router",
    )(x32, whi, wlo, bias)


DMA_UNROLL = 8


GATHER_DEPTH = 3


def _expert_kernel(be_ref, nb_ref, tok_ref, tok_next_ref, tok_next2_ref, x_hbm, wg_ref, wu_ref, wd_ref,
                   o_ref, xbuf, wg16, wu16, wd16, sem, *, blk):
    i = pl.program_id(0)
    nb = nb_ref[0]
    par = lax.rem(i, GATHER_DEPTH)

    def row_copy(r, tok, p):
        return pltpu.make_async_copy(x_hbm.at[pl.ds(tok, 1), :], xbuf.at[p, pl.ds(r, 1), :], sem.at[p])

    def issue(tref, p):
        def body(g, _):
            for u in range(DMA_UNROLL):
                r = g * DMA_UNROLL + u
                row_copy(r, tref[0, 0, r], p).start(priority=u % 2)
            return 0
        lax.fori_loop(0, blk // DMA_UNROLL, body, 0)

    @pl.when((i == 0) | (be_ref[i] != be_ref[jnp.maximum(i - 1, 0)]))
    def _():
        wg16[...] = _bf(wg_ref[...])
        wu16[...] = _bf(wu_ref[...])
        wd16[...] = _bf(wd_ref[...])

    @pl.when(i == 0)
    def _():
        issue(tok_ref, 0)

        @pl.when(1 < nb)
        def _():
            issue(tok_next_ref, 1)

    @pl.when(i + 2 < nb)
    def _():
        issue(tok_next2_ref, lax.rem(i + 2, GATHER_DEPTH))

    @pl.when(i < nb)
    def _():
        def wait(g, _):
            for u in range(DMA_UNROLL):
                row_copy(0, 0, par).wait()
            return 0
        lax.fori_loop(0, blk // DMA_UNROLL, wait, 0)
        xe = _bf(xbuf[par])
        h = _silu(_dot(xe, wg16[...])) * _dot(xe, wu16[...])
        o_ref[...] = _dot(_bf(h), wd16[...])

    @pl.when(i >= nb)
    def _():
        o_ref[...] = jnp.zeros_like(o_ref)


def _experts(x32, slot_tok, block_e, nb_used, wgl, wul, wdl, blk):
    (wg16, l), (wu16, _), (wd16, _) = wgl, wul, wdl
    n_blocks = block_e.shape[0]
    wspec = lambda shape: pl.BlockSpec((None, None) + shape, lambda i, be, nb: (l, be[i], 0, 0))
    tok3 = slot_tok.reshape(n_blocks, 1, blk)
    grid_spec = pltpu.PrefetchScalarGridSpec(
        num_scalar_prefetch=2,
        grid=(n_blocks,),
        in_specs=[pl.BlockSpec((1, 1, blk), lambda i, be, nb: (i, 0, 0), memory_space=pltpu.SMEM),
                  pl.BlockSpec((1, 1, blk), lambda i, be, nb: (jnp.minimum(i + 1, n_blocks - 1), 0, 0),
                               memory_space=pltpu.SMEM),
                  pl.BlockSpec((1, 1, blk), lambda i, be, nb: (jnp.minimum(i + 2, n_blocks - 1), 0, 0),
                               memory_space=pltpu.SMEM),
                  pl.BlockSpec(memory_space=pl.ANY),
                  wspec((D_MODEL, D_EXPERT)), wspec((D_MODEL, D_EXPERT)), wspec((D_EXPERT, D_MODEL))],
        out_specs=pl.BlockSpec((blk, D_MODEL), lambda i, be, nb: (i, 0)),
        scratch_shapes=[pltpu.VMEM((GATHER_DEPTH, blk, D_MODEL), F32),
                        pltpu.VMEM((D_MODEL, D_EXPERT), BF16), pltpu.VMEM((D_MODEL, D_EXPERT), BF16),
                        pltpu.VMEM((D_EXPERT, D_MODEL), BF16), pltpu.SemaphoreType.DMA((GATHER_DEPTH,))],
    )
    return pl.pallas_call(
        functools.partial(_expert_kernel, blk=blk),
        grid_spec=grid_spec,
        out_shape=jax.ShapeDtypeStruct((n_blocks * blk, D_MODEL), F32),
        compiler_params=_cparams(("arbitrary",)),
        name="moe_experts",
    )(block_e, nb_used, tok3, tok3, tok3, x32, wg16, wu16, wd16)


def _combine_kernel(slot_ref, slot_next_ref, y_hbm, x_ref, gates_ref, g_ref, b_ref, o32_ref, o16_ref,
                    ybuf, sem, *, tm):
    i = pl.program_id(0)
    n = pl.num_programs(0)
    par = lax.rem(i, 2)

    def row_copy(r, k, slot, p):
        return pltpu.make_async_copy(y_hbm.at[pl.ds(slot, 1), :], ybuf.at[p, k, pl.ds(r, 1), :], sem.at[p])

    def issue(sref, p):
        def body(g, _):
            for u in range(DMA_UNROLL):
                r = g * DMA_UNROLL + u
                row_copy(r, 0, sref[0, 0, 2 * r], p).start(priority=0)
                row_copy(r, 1, sref[0, 0, 2 * r + 1], p).start(priority=1)
            return 0
        lax.fori_loop(0, tm // DMA_UNROLL, body, 0)

    @pl.when(i == 0)
    def _():
        issue(slot_ref, 0)

    @pl.when(i + 1 < n)
    def _():
        issue(slot_next_ref, 1 - par)

    def wait(g, _):
        for u in range(2 * DMA_UNROLL):
            row_copy(0, 0, 0, par).wait()
        return 0

    lax.fori_loop(0, tm // DMA_UNROLL, wait, 0)
    gates = gates_ref[...]
    y = ybuf[par, 0] * gates[:, 0:1] + ybuf[par, 1] * gates[:, 1:2]
    out = _layer_norm_rows(ALPHA * x_ref[...] + y, g_ref[...], b_ref[...])
    o32_ref[...] = out
    o16_ref[...] = _bf(out)


def _combine_ln(slots, yb, x32, gates, g, b):
    t = x32.shape[0]
    tm = min(256, t)
    n = t // tm
    row = pl.BlockSpec((tm, D_MODEL), lambda i: (i, 0))
    vec = pl.BlockSpec((1, D_MODEL), lambda i: (0, 0))
    slots3 = slots.reshape(n, 1, 2 * tm)
    return pl.pallas_call(
        functools.partial(_combine_kernel, tm=tm),
        grid=(n,),
        in_specs=[pl.BlockSpec((1, 1, 2 * tm), lambda i: (i, 0, 0), memory_space=pltpu.SMEM),
                  pl.BlockSpec((1, 1, 2 * tm), lambda i: (jnp.minimum(i + 1, n - 1), 0, 0),
                               memory_space=pltpu.SMEM),
                  pl.BlockSpec(memory_space=pl.ANY), row,
                  pl.BlockSpec((tm, LANES), lambda i: (i, 0)), vec, vec],
        out_specs=[row, row],
        out_shape=[jax.ShapeDtypeStruct((t, D_MODEL), F32), jax.ShapeDtypeStruct((t, D_MODEL), BF16)],
        scratch_shapes=[pltpu.VMEM((2, 2, tm, D_MODEL), F32), pltpu.SemaphoreType.DMA((2,))],
        compiler_params=_cparams(("arbitrary",)),
        name="moe_combine_ln",
    )(slots3, slots3, yb, x32, gates, g, b)


def _moe_plan(ids, sizes_row, blk):
    t = ids.shape[0]
    sizes = sizes_row[0, N_GROUPS:N_GROUPS + N_EXPERTS].astype(I32)
    padded = (sizes + blk - 1) // blk * blk
    pad_ends = jnp.cumsum(padded)
    pad_starts = pad_ends - padded
    slots = (pad_starts[ids[:, 0:2]] + ids[:, 2:4]).astype(I32)
    n_blocks = 2 * t // blk + N_EXPERTS
    tok_of = jnp.broadcast_to(jnp.arange(t, dtype=I32)[:, None], (t, 2))
    slot_tok = jnp.zeros((n_blocks * blk,), I32).at[slots.reshape(-1)].set(tok_of.reshape(-1))
    starts = jnp.arange(n_blocks, dtype=I32) * blk
    block_e = jnp.minimum(jnp.sum((pad_ends[None, :] <= starts[:, None]).astype(I32), axis=-1),
                          N_EXPERTS - 1).astype(I32)
    nb_used = (pad_ends[-1:] // blk).astype(I32)
    return slots, slot_tok, block_e, nb_used


MOE_BLK = 256
STACKED = ('w16', 'w32', 'wg', 'wb', 'wout', 'eg', 'eu', 'ed')
IN_TM = 1024
IN_TN16, IN_TN32 = 1024, 768


def _prep_weights(w_in, dsa_q_norm, w_uq, gdn_conv, gdn_a_log, gdn_dt_bias, gdn_norm, w_branch,
                  w_branch_gate, w_out, ln1_g, ln1_b, w_router_group, b_router_group,
                  w_router_expert, b_router_expert, w_exp_gate, w_exp_up, w_exp_down, ln2_g, ln2_b):
    depth = w_in.shape[0]
    zc = lambda n: jnp.zeros((depth, D_MODEL, n), F32)
    c = lambda a, b: w_in[..., a:b]
    w16 = jnp.concatenate([c(0, 1536), c(3968, 4224), c(4224, 4288), zc(P16_W - 1856)], -1).astype(BF16)
    w32 = jnp.concatenate([c(4296, 5832), c(1536, 3584), c(5840, 6352), c(3584, 3968),
                           c(4288, 4296), c(5832, 5836), c(5836, 5840), zc(LANES - 16)], -1).astype(BF16)
    qi = w_uq[..., BRANCH_W:].reshape(depth, DSA_Q_RANK, IDX_HEADS, IDX_DIM)
    qi = jnp.pad(qi, ((0, 0), (0, 0), (0, 0), (0, LANES - IDX_DIM))).reshape(depth, DSA_Q_RANK, IDX_HEADS * LANES)
    wuq = jnp.concatenate([qi, w_uq[..., :BRANCH_W]], -1).astype(BF16)
    wr = jnp.concatenate([w_router_group, w_router_expert,
                          zc(LANES - N_GROUPS - N_EXPERTS)], -1)
    rhi = wr.astype(BF16)
    rlo = (wr - rhi.astype(F32)).astype(BF16)
    rb = jnp.concatenate([b_router_group, b_router_expert,
                          jnp.zeros((depth, LANES - N_GROUPS - N_EXPERTS), F32)], -1)[:, None, :]
    prm = jnp.zeros((depth, SUBLANES, LANES), F32)
    prm = prm.at[:, 0, SMALL_A:SMALL_A + N_HEADS].set(gdn_a_log)
    prm = prm.at[:, 1, SMALL_A:SMALL_A + N_HEADS].set(gdn_dt_bias)
    return dict(
        w16=w16, w32=w32, q_norm=dsa_q_norm[:, None, :], wuq=wuq, conv=gdn_conv, prm=prm,
        gnorm=gdn_norm[:, None, :], wb=w_branch.astype(BF16), wg=w_branch_gate.astype(BF16),
        wout=w_out.astype(BF16), ln1g=ln1_g[:, None, :], ln1b=ln1_b[:, None, :], rhi=rhi, rlo=rlo, rb=rb,
        eg=w_exp_gate, eu=w_exp_up, ed=w_exp_down,
        ln2g=ln2_g[:, None, :], ln2b=ln2_b[:, None, :])


def _mixer(x16, lw, batch, seq):
    p16 = _matmul(x16, lw['w16'], BF16, IN_TM, IN_TN16, "in_proj_bf16")
    p32 = _matmul(x16, lw['w32'], F32, IN_TM, IN_TN32, "in_proj_f32")
    y_a = _sb_attention(p16, batch, seq)
    y_b = _retention(p32, batch, seq)
    q_all = _dsa_query(p32, lw['q_norm'], lw['wuq'])
    y_c = _dsa_attention(q_all, p32, p16, batch, seq)
    gq = _gdn_conv(p32, lw['conv'], batch, seq)
    y_d = _gated_deltanet(gq, p32, lw['prm'], lw['gnorm'], batch, seq)
    return _merge(x16, (y_a, y_b, y_c, y_d), lw['wg'], lw['wb'])


def _moe(x32, lw):
    ids, gates, sizes = _router(x32, lw['rhi'], lw['rlo'], lw['rb'])
    slots, slot_tok, block_e, nb_used = _moe_plan(ids, sizes, MOE_BLK)
    yb = _experts(x32, slot_tok, block_e, nb_used, lw['eg'], lw['eu'], lw['ed'], MOE_BLK)
    return _combine_ln(slots, yb, x32, gates, lw['ln2g'], lw['ln2b'])


def _layer_weights(ws, l):
    return {k: ((v, l) if k in STACKED else v[l]) for k, v in ws.items()}


def _layer(x32, x16, lw, batch, seq):
    merged = _mixer(x16, lw, batch, seq)
    x32, x16 = _outproj_ln(merged, lw['wout'], x32, lw['ln1g'], lw['ln1b'])
    return _moe(x32, lw)


def kernel(x, w_in, dsa_q_norm, w_uq, gdn_conv, gdn_a_log, gdn_dt_bias, gdn_norm, w_branch, w_branch_gate,
           w_out, ln1_g, ln1_b, w_router_group, b_router_group, w_router_expert, b_router_expert,
           w_exp_gate, w_exp_up, w_exp_down, ln2_g, ln2_b):
    batch, seq, d = x.shape
    assert d == D_MODEL and seq % RET_CHUNK == 0
    ws = _prep_weights(w_in, dsa_q_norm, w_uq, gdn_conv, gdn_a_log, gdn_dt_bias, gdn_norm, w_branch,
                       w_branch_gate, w_out, ln1_g, ln1_b, w_router_group, b_router_group,
                       w_router_expert, b_router_expert, w_exp_gate, w_exp_up, w_exp_down, ln2_g, ln2_b)
    x32 = x.reshape(batch * seq, d)
    x16 = x32.astype(BF16)
    for l in range(w_in.shape[0]):
        x32, x16 = _layer(x32, x16, _layer_weights(ws, l), batch, seq)
    return x32.reshape(batch, seq, d)
```
